```python
import math
import jax, jax.numpy as jnp
from jax import lax
import numpy as np

D_MODEL = 1024
BATCH = 8
SEQ = 2048
DEPTH = 4
DEC_BATCH = 128
DEC_SEQ = 4
PAST_LEN = 8192
PAGE_SIZE = 128

F32 = jnp.float32
MIX_WIDTH = D_MODEL
N_GROUPS = 4
GROUP_WIDTH = MIX_WIDTH // N_GROUPS
A_HEADS = 4
A_HEAD_DIM = GROUP_WIDTH // A_HEADS // 2
A_V_DIM = 2 * A_HEAD_DIM
B_HEADS = 4
B_HEAD_DIM = GROUP_WIDTH // B_HEADS
IDX_HEADS = 4
IDX_DIM = 64
DSA_TOPK = 256
C_HEADS = 4
C_HEAD_DIM = GROUP_WIDTH // C_HEADS
MOBA_BLOCK = 256
MOBA_TOPK = 3
D_HEADS = 4
D_NOPE = 64
D_ROPE = 32
D_V = GROUP_WIDTH // D_HEADS
Q_LORA = 256
KV_LORA = 128
D_FF = -(-8 * D_MODEL // (3 * 256)) * 256
ROPE_THETA = 500000.0
ROPE_FRACTION = 4
Q_BLOCK = 128
MOBA_Q_BLOCK = 32
EPS = 1e-6

CACHE_A_W = 2 * A_HEAD_DIM + A_V_DIM
CACHE_B_W = 2 * B_HEAD_DIM + IDX_DIM
CACHE_C_W = 2 * C_HEAD_DIM
CACHE_D_W = KV_LORA + D_ROPE

IN_SIZES = (A_HEADS * 2 * A_HEAD_DIM, 2 * A_HEAD_DIM, A_V_DIM,
            B_HEADS * B_HEAD_DIM, B_HEAD_DIM, B_HEAD_DIM,
            IDX_HEADS * IDX_DIM, IDX_DIM, IDX_HEADS,
            C_HEADS * C_HEAD_DIM, C_HEAD_DIM, C_HEAD_DIM,
            Q_LORA, KV_LORA, D_ROPE)
IN_WIDTH = sum(IN_SIZES)
SPLIT_POINTS = tuple(int(v) for v in np.cumsum(IN_SIZES)[:-1])

kernel_name = 'hybrid_diff_dsa_moba_mla_step'


def rms_norm(x, g):
    xf = x.astype(F32)
    y = xf * lax.rsqrt(jnp.mean(xf * xf, axis=-1, keepdims=True) + EPS)
    return (y * g.astype(F32)).astype(x.dtype)


def rope(x, pos):
    d = x.shape[-1]
    half = d // 2
    inv_freq = ROPE_THETA ** (-(jnp.arange(half, dtype=F32) * 2.0 / d))
    ang = pos.astype(F32)[:, None] * inv_freq[None, :]
    ang = ang.reshape((ang.shape[0],) + (1,) * (x.ndim - 3) + (half,))
    cos, sin = jnp.cos(ang), jnp.sin(ang)
    x1 = x[..., :half].astype(F32)
    x2 = x[..., half:].astype(F32)
    return jnp.concatenate([x1 * cos - x2 * sin, x2 * cos + x1 * sin], axis=-1).astype(x.dtype)


def partial_rope(x, pos):
    r = x.shape[-1] // ROPE_FRACTION
    return jnp.concatenate([rope(x[..., :r], pos), x[..., r:]], axis=-1)


def sweep_queries(fn, block, q_pos, *qs):
    t = q_pos.shape[0]
    if t <= block or t % block != 0:
        return fn(q_pos, *qs)
    n = t // block
    pos_b = q_pos.reshape(n, block)
    qs_b = tuple(jnp.swapaxes(q.reshape((q.shape[0], n, block) + q.shape[2:]), 0, 1) for q in qs)
    out = lax.map(lambda args: fn(*args), (pos_b,) + qs_b)
    out = jnp.swapaxes(out, 0, 1)
    return out.reshape((out.shape[0], t) + out.shape[3:])


def diff_attention(q, k, v, q_pos, lam, lam_init, sub_g):
    nb, t = q.shape[0], q.shape[1]
    key_pos = jnp.arange(k.shape[1], dtype=jnp.int32)
    scale = A_HEAD_DIM ** -0.5

    def block(qp, qb):
        s = jnp.einsum('bthmd,bsmd->bmhts', qb, k).astype(F32) * scale
        s = jnp.where(key_pos[None, :] <= qp[:, None], s, -jnp.inf)
        p = jax.nn.softmax(s, axis=-1)
        a = p[:, 0] - lam * p[:, 1]
        return jnp.einsum('bhts,bsv->bthv', a.astype(v.dtype), v)

    o = sweep_queries(block, Q_BLOCK, q_pos, q)
    o = rms_norm(o, sub_g) * (1.0 - lam_init)
    return o.reshape(nb, t, A_HEADS * A_V_DIM)


def dsa_attention(q, k, v, iq, ik, iw, q_pos):
    nb, t = q.shape[0], q.shape[1]
    n_keys = k.shape[1]
    n_sel = min(DSA_TOPK, n_keys // 4)
    key_pos = jnp.arange(n_keys, dtype=jnp.int32)
    take = jax.vmap(lambda rows, idx: rows[idx])

    def block(qp, qb, iqb, iwb):
        rel = jax.nn.relu(jnp.einsum('bthd,bsd->bths', iqb, ik).astype(F32) * IDX_DIM ** -0.5)
        score = jnp.einsum('bth,bths->bts', iwb.astype(F32) * IDX_HEADS ** -0.5, rel)
        score = jnp.where((key_pos[None, :] <= qp[:, None])[None], score, -jnp.inf)
        _, idx = lax.top_k(score, n_sel)
        valid = idx <= qp[None, :, None]
        kg = take(k, idx)
        vg = take(v, idx)
        s = jnp.einsum('bthd,btnd->bthn', qb, kg).astype(F32) * B_HEAD_DIM ** -0.5
        s = jnp.where(valid[:, :, None, :], s, -jnp.inf)
        p = jax.nn.softmax(s, axis=-1)
        return jnp.einsum('bthn,btnd->bthd', p.astype(vg.dtype), vg)

    o = sweep_queries(block, Q_BLOCK, q_pos, q, iq, iw)
    return o.reshape(nb, t, B_HEADS * B_HEAD_DIM)


def moba_attention(q, k, v, q_pos):
    nb, t = q.shape[0], q.shape[1]
    n_keys = k.shape[1]
    n_blk = -(-n_keys // MOBA_BLOCK)
    pad = n_blk * MOBA_BLOCK - n_keys
    kb = jnp.pad(k, ((0, 0), (0, pad), (0, 0))).reshape(nb, n_blk, MOBA_BLOCK, C_HEAD_DIM)
    vb = jnp.pad(v, ((0, 0), (0, pad), (0, 0))).reshape(nb, n_blk, MOBA_BLOCK, C_HEAD_DIM)
    k_mean = jnp.mean(kb.astype(F32), axis=2)
    n_sel = min(MOBA_TOPK, n_blk)
    blk_ids = jnp.arange(n_blk, dtype=jnp.int32)
    in_blk = jnp.arange(MOBA_BLOCK, dtype=jnp.int32)
    scale = C_HEAD_DIM ** -0.5
    take = jax.vmap(lambda blocks, idx: blocks[idx])

    def block(qp, qb):
        tb = qp.shape[0]
        own = qp // MOBA_BLOCK
        g = jnp.einsum('bthd,bjd->bthj', qb.astype(F32), k_mean)
        g = jnp.where((blk_ids[None, :] < own[:, None])[None, :, None, :], g, -jnp.inf)
        _, sel = lax.top_k(g, n_sel)
        sel_ok = sel < own[None, :, None, None]
        kg = take(kb, sel)
        vg = take(vb, sel)
        s_past = jnp.einsum('bthd,bthnkd->bthnk', qb, kg).astype(F32) * scale
        s_past = jnp.where(sel_ok[..., None], s_past, -jnp.inf).reshape(nb, tb, C_HEADS, n_sel * MOBA_BLOCK)
        ko = kb[:, own]
        vo = vb[:, own]
        s_own = jnp.einsum('bthd,btkd->bthk', qb, ko).astype(F32) * scale
        own_pos = own[:, None] * MOBA_BLOCK + in_blk[None, :]
        s_own = jnp.where((own_pos <= qp[:, None])[None, :, None, :], s_own, -jnp.inf)
        p = jax.nn.softmax(jnp.concatenate([s_past, s_own], axis=-1), axis=-1)
        p_past = p[..., :n_sel * MOBA_BLOCK].reshape(nb, tb, C_HEADS, n_sel, MOBA_BLOCK)
        p_own = p[..., n_sel * MOBA_BLOCK:]
        return (jnp.einsum('bthnk,bthnkd->bthd', p_past.astype(vg.dtype), vg)
                + jnp.einsum('bthk,btkd->bthd', p_own.astype(vo.dtype), vo))

    o = sweep_queries(block, MOBA_Q_BLOCK, q_pos, q)
    return o.reshape(nb, t, C_HEADS * C_HEAD_DIM)


def mla_attention(q_nope, q_rope, ckv, krope, w_uk, w_uv, q_pos):
    nb, t = q_nope.shape[0], q_nope.shape[1]
    key_pos = jnp.arange(ckv.shape[1], dtype=jnp.int32)
    scale = (D_NOPE + D_ROPE) ** -0.5
    q_abs = jnp.einsum('bthn,chn->bthc', q_nope, w_uk)

    def block(qp, qa, qr):
        s = jnp.einsum('bthc,bsc->bhts', qa, ckv) + jnp.einsum('bthr,bsr->bhts', qr, krope)
        s = jnp.where(key_pos[None, :] <= qp[:, None], s.astype(F32) * scale, -jnp.inf)
        p = jax.nn.softmax(s, axis=-1)
        return jnp.einsum('bhts,bsc->bthc', p.astype(ckv.dtype), ckv)

    o_lat = sweep_queries(block, Q_BLOCK, q_pos, q_abs, q_rope)
    o = jnp.einsum('bthc,chv->bthv', o_lat, w_uv)
    return o.reshape(nb, t, D_HEADS * D_V)


def gather_pages(cache_l, page_table):
    pages = cache_l[page_table]
    return pages.reshape(page_table.shape[0], -1, cache_l.shape[-1])


def decoder_layer(x, pos, past, lam_init, attn_g, w_in_l, a_lam, a_sub_g, dq_g, dkv_g,
                  w_uq, w_uk, w_uv, w_out_l, ffn_g, w_gate_l, w_up_l, w_down_l):
    nb, t = x.shape[0], x.shape[1]
    h = rms_norm(x, attn_g)
    (aq, ak, av, bq, bk, bv, iq, ik, iw, cq, ck, cv, dcq, dckv, dkr) = jnp.split(h @ w_in_l, SPLIT_POINTS, axis=-1)
    aq = partial_rope(aq.reshape(nb, t, A_HEADS, 2, A_HEAD_DIM), pos)
    ak = partial_rope(ak.reshape(nb, t, 2, A_HEAD_DIM), pos)
    row_a = jnp.concatenate([ak.reshape(nb, t, 2 * A_HEAD_DIM), av], axis=-1)
    bq = partial_rope(bq.reshape(nb, t, B_HEADS, B_HEAD_DIM), pos)
    iq = partial_rope(iq.reshape(nb, t, IDX_HEADS, IDX_DIM), pos)
    row_b = jnp.concatenate([partial_rope(bk, pos), bv, partial_rope(ik, pos)], axis=-1)
    cq = partial_rope(cq.reshape(nb, t, C_HEADS, C_HEAD_DIM), pos)
    row_c = jnp.concatenate([partial_rope(ck, pos), cv], axis=-1)
    q_d = (rms_norm(dcq, dq_g) @ w_uq).reshape(nb, t, D_HEADS, D_NOPE + D_ROPE)
    qd_nope = q_d[..., :D_NOPE]
    qd_rope = rope(q_d[..., D_NOPE:], pos)
    row_d = jnp.concatenate([rms_norm(dckv, dkv_g), rope(dkr, pos)], axis=-1)
    rows = (row_a, row_b, row_c, row_d)
    if past is None:
        full = rows
    else:
        full = tuple(jnp.concatenate([p, r], axis=1) for p, r in zip(past, rows))
    fa, fb, fc, fd = full
    n_keys = fa.shape[1]
    lam = (jnp.exp(jnp.sum(a_lam[0].astype(F32) * a_lam[1].astype(F32)))
           - jnp.exp(jnp.sum(a_lam[2].astype(F32) * a_lam[3].astype(F32))) + lam_init)
    o_a = diff_attention(aq, fa[..., :2 * A_HEAD_DIM].reshape(nb, n_keys, 2, A_HEAD_DIM),
                         fa[..., 2 * A_HEAD_DIM:], pos, lam, lam_init, a_sub_g)
    o_b = dsa_attention(bq, fb[..., :B_HEAD_DIM], fb[..., B_HEAD_DIM:2 * B_HEAD_DIM],
                        iq, fb[..., 2 * B_HEAD_DIM:], iw, pos)
    o_c = moba_attention(cq, fc[..., :C_HEAD_DIM], fc[..., C_HEAD_DIM:], pos)
    o_d = mla_attention(qd_nope, qd_rope, fd[..., :KV_LORA], fd[..., KV_LORA:], w_uk, w_uv, pos)
    x = x + jnp.concatenate([o_a, o_b, o_c, o_d], axis=-1) @ w_out_l
    h = rms_norm(x, ffn_g)
    x = x + (jax.nn.silu(h @ w_gate_l) * (h @ w_up_l)) @ w_down_l
    return x, rows


def setup_inputs(seed: int = 0) -> dict:
    key = jax.random.key(seed)
    ks = jax.random.split(key, 24)
    n_pages = PAST_LEN // PAGE_SIZE
    n_used = DEC_BATCH * n_pages
    n_pool = n_used + n_used // 4

    def nrm(k, shape, scale=1.0):
        return jax.random.normal(k, shape, F32) * scale

    page_table = jax.random.permutation(ks[6], n_pool)[:n_used].reshape(DEC_BATCH, n_pages).astype(jnp.int32)
    return {
        'x_prompt': nrm(ks[0], (BATCH, SEQ, D_MODEL)),
        'x_sample': nrm(ks[1], (DEC_BATCH, DEC_SEQ, D_MODEL)),
        'cache_diff_kv': nrm(ks[2], (DEPTH, n_pool, PAGE_SIZE, CACHE_A_W)),
        'cache_dsa_kv': nrm(ks[3], (DEPTH, n_pool, PAGE_SIZE, CACHE_B_W)),
        'cache_moba_kv': nrm(ks[4], (DEPTH, n_pool, PAGE_SIZE, CACHE_C_W)),
        'cache_mla_latent': nrm(ks[5], (DEPTH, n_pool, PAGE_SIZE, CACHE_D_W)),
        'page_table': page_table,
        'attn_norm_g': 1.0 + nrm(ks[7], (DEPTH, D_MODEL), 0.02),
        'w_in': nrm(ks[8], (DEPTH, D_MODEL, IN_WIDTH), D_MODEL ** -0.5),
        'a_lambda': nrm(ks[9], (DEPTH, 4, A_HEAD_DIM), 0.1),
        'a_subln_g': 1.0 + nrm(ks[10], (DEPTH, A_V_DIM), 0.02),
        'd_q_norm_g': 1.0 + nrm(ks[11], (DEPTH, Q_LORA), 0.02),
        'd_kv_norm_g': 1.0 + nrm(ks[12], (DEPTH, KV_LORA), 0.02),
        'd_w_uq': nrm(ks[13], (DEPTH, Q_LORA, D_HEADS * (D_NOPE + D_ROPE)), Q_LORA ** -0.5),
        'd_w_uk': nrm(ks[14], (DEPTH, KV_LORA, D_HEADS, D_NOPE), KV_LORA ** -0.5),
        'd_w_uv': nrm(ks[15], (DEPTH, KV_LORA, D_HEADS, D_V), KV_LORA ** -0.5),
        'w_out': nrm(ks[16], (DEPTH, MIX_WIDTH, D_MODEL), MIX_WIDTH ** -0.5),
        'ffn_norm_g': 1.0 + nrm(ks[17], (DEPTH, D_MODEL), 0.02),
        'w_gate': nrm(ks[18], (DEPTH, D_MODEL, D_FF), D_MODEL ** -0.5),
        'w_up': nrm(ks[19], (DEPTH, D_MODEL, D_FF), D_MODEL ** -0.5),
        'w_down': nrm(ks[20], (DEPTH, D_FF, D_MODEL), D_FF ** -0.5),
        'final_norm_g': 1.0 + nrm(ks[21], (D_MODEL,), 0.02),
    }


def reference(x_prompt, x_sample, cache_diff_kv, cache_dsa_kv, cache_moba_kv, cache_mla_latent, page_table,
              attn_norm_g, w_in, a_lambda, a_subln_g, d_q_norm_g, d_kv_norm_g, d_w_uq, d_w_uk, d_w_uv,
              w_out, ffn_norm_g, w_gate, w_up, w_down, final_norm_g):
    past_len = page_table.shape[1] * cache_diff_kv.shape[2]
    pos_p = jnp.arange(x_prompt.shape[1], dtype=jnp.int32)
    pos_s = past_len + jnp.arange(x_sample.shape[1], dtype=jnp.int32)
    caches = (cache_diff_kv, cache_dsa_kv, cache_moba_kv, cache_mla_latent)
    hp, hs = x_prompt, x_sample
    rows_p, rows_s = [], []
    for l in range(DEPTH):
        lam_init = 0.8 - 0.6 * math.exp(-0.3 * l)
        lw = (attn_norm_g[l], w_in[l], a_lambda[l], a_subln_g[l], d_q_norm_g[l], d_kv_norm_g[l],
              d_w_uq[l], d_w_uk[l], d_w_uv[l], w_out[l], ffn_norm_g[l], w_gate[l], w_up[l], w_down[l])
        hp, rp = decoder_layer(hp, pos_p, None, lam_init, *lw)
        past = tuple(gather_pages(c[l], page_table) for c in caches)
        hs, rs = decoder_layer(hs, pos_s, past, lam_init, *lw)
        rows_p.append(rp)
        rows_s.append(rs)
    y_prompt = rms_norm(hp, final_norm_g)
    y_sample = rms_norm(hs, final_norm_g)
    new_diff_kv_prompt = jnp.stack([r[0] for r in rows_p])
    new_dsa_kv_prompt = jnp.stack([r[1] for r in rows_p])
    new_moba_kv_prompt = jnp.stack([r[2] for r in rows_p])
    new_mla_latent_prompt = jnp.stack([r[3] for r in rows_p])
    new_diff_kv_sample = jnp.stack([r[0] for r in rows_s])
    new_dsa_kv_sample = jnp.stack([r[1] for r in rows_s])
    new_moba_kv_sample = jnp.stack([r[2] for r in rows_s])
    new_mla_latent_sample = jnp.stack([r[3] for r in rows_s])
    return (y_prompt, y_sample, new_diff_kv_prompt, new_dsa_kv_prompt, new_moba_kv_prompt, new_mla_latent_prompt,
            new_diff_kv_sample, new_dsa_kv_sample, new_moba_kv_sample, new_mla_latent_sample)
```

```python
import functools
import math

import numpy as np
import jax
import jax.numpy as jnp
from jax import lax
from jax.experimental import pallas as pl
from jax.experimental.pallas import tpu as pltpu

F32, BF16, I32 = jnp.float32, jnp.bfloat16, jnp.int32

N_HEADS = 4
GROUP_WIDTH = 256
HEAD_W = 64
A_HEAD_DIM = 32
IDX_DIM = 64
DSA_TOPK = 256
MOBA_BLOCK = 256
MOBA_TOPK = 3
D_NOPE, D_ROPE, KV_LORA, Q_LORA = 64, 32, 128, 256
ROPE_THETA = 500000.0
EPS = 1e-6
CACHE_W = (128, 192, 128, 160)

_NAMES = ('aq', 'ak', 'av', 'bq', 'bk', 'bv', 'iq', 'ik', 'iw', 'cq', 'ck', 'cv', 'dcq', 'dckv', 'dkr')
_SIZES = (256, 64, 64, 256, 64, 64, 256, 64, 4, 256, 64, 64, 256, 128, 32)
_ORIG = dict(zip(_NAMES, (int(v) for v in np.cumsum((0,) + _SIZES[:-1]))))
IN_WIDTH = sum(_SIZES)
_NEW = {'aq': 0, 'bq': 256, 'iq': 512, 'cq': 768, 'dcq': 1024, 'ak': 1280, 'av': 1344, 'ck': 1408, 'cv': 1472,
        'bk': 1536, 'bv': 1600, 'ik': 1664, 'iw': 1728, 'dckv': 1792, 'dkr': 1920}
WP = 2048
_ROT = {'aq': (32, 8), 'ak': (32, 8), 'bq': (64, 16), 'bk': (64, 16), 'iq': (64, 16), 'ik': (64, 16),
        'cq': (64, 16), 'ck': (64, 16), 'dkr': (32, 32)}
_ROT_DIMS = (8, 16, 32)
_FBASE = {8: 0, 16: 4, 32: 12}
QD_W = N_HEADS * (D_NOPE + D_ROPE)
QX_W = 256

LANE = 128
INT_MIN = -2 ** 31
NEG = -1e30
BIG = 2 ** 30
VMEM_LIMIT = 56 * 1024 * 1024


def _build_layout():
    src = np.full(WP, IN_WIDTH, np.int32)
    swp = np.full(WP, IN_WIDTH, np.int32)
    fidx = np.zeros(WP, np.int32)
    sgn = np.zeros(WP, np.float32)
    for name, size in zip(_NAMES, _SIZES):
        for j in range(size):
            c, o = _NEW[name] + j, _ORIG[name] + j
            src[c] = o
            if name in _ROT:
                gw, r = _ROT[name]
                d, half = j % gw, r // 2
                if d < r:
                    first = d < half
                    swp[c] = o + half if first else o - half
                    fidx[c] = _FBASE[r] + d % half
                    sgn[c] = -1.0 if first else 1.0
    return src, swp, fidx, sgn


def _build_q_layout():
    per = D_NOPE + D_ROPE
    src = np.zeros(QD_W, np.int32)
    swp = np.full(QD_W, QD_W, np.int32)
    fidx = np.zeros(QD_W, np.int32)
    sgn = np.zeros(QD_W, np.float32)
    half = D_ROPE // 2
    for h in range(N_HEADS):
        for n in range(D_NOPE):
            src[h * D_NOPE + n] = h * per + n
        for r in range(D_ROPE):
            c, o = N_HEADS * D_NOPE + h * D_ROPE + r, h * per + D_NOPE + r
            src[c] = o
            first = r < half
            swp[c] = o + half if first else o - half
            fidx[c] = _FBASE[D_ROPE] + r % half
            sgn[c] = -1.0 if first else 1.0
    return src, swp, fidx, sgn


def _rope_tables(pos, fidx, sgn):
    inv = jnp.concatenate([ROPE_THETA ** (-(jnp.arange(r // 2, dtype=F32) * 2.0 / r)) for r in _ROT_DIMS])
    ang = pos.astype(F32)[:, None] * inv[None, :]
    cos_f, sin_f = jnp.cos(ang), jnp.sin(ang)
    rot = jnp.asarray(sgn != 0)
    cos = jnp.where(rot[None, :], cos_f[:, fidx], 1.0)
    sin = sin_f[:, fidx] * jnp.asarray(sgn)[None, :]
    return cos, sin


def _dot(a, b):
    return jnp.dot(a, b, preferred_element_type=F32)


def _dot_nt(a, b):
    return lax.dot_general(a, b, (((1,), (1,)), ((), ())), preferred_element_type=F32)


def _group_mask(width, group, idx):
    lane = lax.broadcasted_iota(I32, (1, width), 1)
    return (lane // group) == idx


def _tile4(x, base):
    w = x.shape[1]
    j = lax.broadcasted_iota(I32, (w, GROUP_WIDTH), 0)
    c = lax.broadcasted_iota(I32, (w, GROUP_WIDTH), 1)
    rep = jnp.where(j == base + (c % HEAD_W), 1.0, 0.0).astype(BF16)
    return _dot(x, rep).astype(BF16)


def _split_bf16(x):
    hi = x.astype(BF16)
    lo = (x - hi.astype(F32)).astype(BF16)
    return hi, lo


def _rms(x, g):
    return x * lax.rsqrt(jnp.mean(x * x, axis=-1, keepdims=True) + EPS) * g


def _sort_key(x):
    b = lax.bitcast_convert_type(x, I32)
    return b ^ ((b >> 31) & 0x7FFFFFFF)


def _lane_sum128(m):
    acc = m[:, 0:LANE]
    for j in range(1, m.shape[1] // LANE):
        acc = acc + m[:, j * LANE:(j + 1) * LANE]
    return acc


def _flash(qh, k_ref, v_ref, nkv, tk, scale, mask_fn):
    tq = qh.shape[0]
    wv = v_ref.shape[-1]

    def body(c, carry):
        m, l, acc = carry
        off = pl.multiple_of(c * tk, tk)
        s = _dot_nt(qh, k_ref[pl.ds(off, tk), :]) * scale
        s = jnp.where(mask_fn(c), s, NEG)
        m_new = jnp.maximum(m, jnp.max(s, axis=-1, keepdims=True))
        alpha = jnp.exp(m - m_new)
        p = jnp.exp(s - m_new)
        l = alpha * l + jnp.sum(p, axis=-1, keepdims=True)
        acc = alpha * acc + _dot(p.astype(BF16), v_ref[pl.ds(off, tk), :])
        return m_new, l, acc

    init = (jnp.full((tq, 1), NEG, F32), jnp.zeros((tq, 1), F32), jnp.zeros((tq, wv), F32))
    _, l, acc = lax.fori_loop(0, nkv, body, init)
    return acc / l


def _causal_fn(qi, tq, tk):
    qpos = qi * tq + lax.broadcasted_iota(I32, (tq, 1), 0)
    lane = lax.broadcasted_iota(I32, (1, tk), 1)
    return lambda c: (c * tk + lane) <= qpos


def _top_blocks(g, own, nsel):
    lane = lax.broadcasted_iota(I32, g.shape, 1)
    lane_f = lane.astype(F32)
    past = lane < own
    g = jnp.where(past, g, -jnp.inf)
    sel = jnp.zeros(g.shape, jnp.bool_)
    for _ in range(nsel):
        gm = jnp.where(sel, -jnp.inf, g)
        top = jnp.max(gm, axis=-1, keepdims=True)
        first = jnp.min(jnp.where(gm == top, lane_f, float(LANE)), axis=-1, keepdims=True)
        sel = sel | (lane_f == first)
    return sel & past


def _inproj_kernel(x_ref, g_ref, w_ref, cos_ref, sin_ref, gq_ref, gkv_ref, wuq_ref, cosq_ref, sinq_ref, bd_ref,
                   aq_o, bq_o, iq_o, cq_o, qd_o, iw_o, ra_o, rb_o, rc_o, rd_o, ka_o, kb_o, kc_o, kd_o):
    h = _rms(x_ref[...], g_ref[...]).astype(BF16)
    y2 = _dot(h, w_ref[...])
    y = y2[:, :WP] * cos_ref[...] + y2[:, WP:] * sin_ref[...]
    aq_o[...] = y[:, 0:256].astype(BF16)
    bq_o[...] = y[:, 256:512].astype(BF16)
    iq_o[...] = y[:, 512:768].astype(BF16)
    cq_o[...] = y[:, 768:1024].astype(BF16)
    dn = _rms(y[:, 1024:1280], gq_ref[...]).astype(BF16)
    q2 = _dot(dn, wuq_ref[...])
    qd = q2[:, :QD_W] * cosq_ref[...] + q2[:, QD_W:] * sinq_ref[...]
    qd_o[...] = _dot(qd.astype(BF16), bd_ref[...]).astype(BF16)
    ra = y[:, 1280:1408]
    ra_o[...] = ra
    ka_o[...] = ra.astype(BF16)
    rc = y[:, 1408:1536]
    rc_o[...] = rc
    kc_o[...] = rc.astype(BF16)
    rb = y[:, 1536:1792]
    rb_o[...] = rb[:, :192]
    kb_o[...] = rb.astype(BF16)
    iw_o[...] = rb[:, 128:256]
    rd = y[:, 1792:2048]
    ckv = _rms(rd[:, :KV_LORA], gkv_ref[...])
    rd_o[:, 0:KV_LORA] = ckv
    rd_o[:, KV_LORA:KV_LORA + D_ROPE] = rd[:, KV_LORA:KV_LORA + D_ROPE]
    kd_o[:, 0:KV_LORA] = ckv.astype(BF16)
    kd_o[:, KV_LORA:QX_W] = rd[:, KV_LORA:QX_W].astype(BF16)


_K1_OUT = (('aq', 256, BF16), ('bq', 256, BF16), ('iq', 256, BF16), ('cq', 256, BF16), ('qd', N_HEADS * QX_W, BF16),
           ('iw', 128, F32), ('ra', 128, F32), ('rb', 192, F32), ('rc', 128, F32), ('rd', 160, F32),
           ('ka', 128, BF16), ('kb', 256, BF16), ('kc', 128, BF16), ('kd', 256, BF16))


def _token_tile(n):
    return 256 if n % 256 == 0 else n


def _inproj(x2d, tabs, lw, ntab):
    n, d = x2d.shape
    tm = _token_tile(n)
    cos, sin, cosq, sinq = tabs
    row = lambda i: (i, 0)
    tab = lambda i: (i % ntab, 0)
    const = lambda i: (0, 0)
    one = pl.Buffered(1)
    in_specs = [
        pl.BlockSpec((tm, d), row),
        pl.BlockSpec((1, d), const),
        pl.BlockSpec((d, 2 * WP), const, pipeline_mode=one),
        pl.BlockSpec((tm, WP), tab),
        pl.BlockSpec((tm, WP), tab),
        pl.BlockSpec((1, Q_LORA), const),
        pl.BlockSpec((1, KV_LORA), const),
        pl.BlockSpec((Q_LORA, 2 * QD_W), const, pipeline_mode=one),
        pl.BlockSpec((tm, QD_W), tab),
        pl.BlockSpec((tm, QD_W), tab),
        pl.BlockSpec((QD_W, N_HEADS * QX_W), const, pipeline_mode=one),
    ]
    outs = pl.pallas_call(
        _inproj_kernel,
        grid=(n // tm,),
        in_specs=in_specs,
        out_specs=[pl.BlockSpec((tm, w), row) for _, w, _ in _K1_OUT],
        out_shape=[jax.ShapeDtypeStruct((n, w), dt) for _, w, dt in _K1_OUT],
        compiler_params=pltpu.CompilerParams(dimension_semantics=("arbitrary",), vmem_limit_bytes=VMEM_LIMIT),
        name="inproj",
    )(x2d, lw['attn_g'], lw['w_cat'], cos, sin, lw['gq'], lw['gkv'], lw['wuq'], cosq, sinq, lw['bd'])
    return {name: o for (name, _, _), o in zip(_K1_OUT, outs)}


def _ffn_kernel(x_ref, oa_ref, ob_ref, oc_ref, od_ref, wo_ref, g_ref, wg_ref, wu_ref, wd_ref, *rest, final):
    if final:
        gf_ref, x_out, y_out = rest
    else:
        (x_out,) = rest
    x = x_ref[...]
    for i, o_ref in enumerate((oa_ref, ob_ref, oc_ref, od_ref)):
        x = x + _dot(o_ref[...], wo_ref[i * GROUP_WIDTH:(i + 1) * GROUP_WIDTH, :])
    h = _rms(x, g_ref[...]).astype(BF16)
    gate = _dot(h, wg_ref[...])
    up = _dot(h, wu_ref[...])
    act = gate * (1.0 / (1.0 + jnp.exp(-gate))) * up
    x = x + _dot(act.astype(BF16), wd_ref[...])
    x_out[...] = x
    if final:
        y_out[...] = _rms(x, gf_ref[...])


def _ffn(x2d, o_parts, lw, final_g):
    n, d = x2d.shape
    tm = _token_tile(n)
    dff = lw['wg'].shape[1]
    final = final_g is not None
    row = lambda i: (i, 0)
    const = lambda i: (0, 0)
    one = pl.Buffered(1)
    in_specs = [pl.BlockSpec((tm, d), row)] + [pl.BlockSpec((tm, GROUP_WIDTH), row)] * 4 + [
        pl.BlockSpec((d, d), const, pipeline_mode=one),
        pl.BlockSpec((1, d), const),
        pl.BlockSpec((d, dff), const, pipeline_mode=one),
        pl.BlockSpec((d, dff), const, pipeline_mode=one),
        pl.BlockSpec((dff, d), const, pipeline_mode=one),
    ]
    args = [x2d, *o_parts, lw['wo'], lw['ffn_g'], lw['wg'], lw['wu'], lw['wd']]
    out_specs = [pl.BlockSpec((tm, d), row)]
    out_shape = [jax.ShapeDtypeStruct((n, d), F32)]
    if final:
        in_specs.append(pl.BlockSpec((1, d), const))
        args.append(final_g)
        out_specs.append(pl.BlockSpec((tm, d), row))
        out_shape.append(jax.ShapeDtypeStruct((n, d), F32))
    outs = pl.pallas_call(
        functools.partial(_ffn_kernel, final=final),
        grid=(n // tm,),
        in_specs=in_specs,
        out_specs=out_specs,
        out_shape=out_shape,
        compiler_params=pltpu.CompilerParams(dimension_semantics=("arbitrary",), vmem_limit_bytes=VMEM_LIMIT),
        name="outproj_ffn",
    )(*args)
    return outs if final else (outs[0], None)


def _num_chunks(qi, tq, tk):
    return ((qi + 1) * tq + tk - 1) // tk


def _lambda_value(al_ref, lam_init):
    al = al_ref[...]
    s1 = jnp.sum(al[0:1, :] * al[1:2, :], axis=-1, keepdims=True)
    s2 = jnp.sum(al[2:3, :] * al[3:4, :], axis=-1, keepdims=True)
    return jnp.exp(s1) - jnp.exp(s2) + lam_init


def _subln(o, g, lam_init, width, first_lane):
    msb = jnp.zeros(o.shape, F32)
    for hh in range(first_lane // HEAD_W, width // HEAD_W):
        mh = _group_mask(width, HEAD_W, hh)
        ms = jnp.sum(jnp.where(mh, o * o, 0.0), axis=-1, keepdims=True) * (1.0 / HEAD_W)
        msb = msb + jnp.where(mh, ms, 0.0)
    return o * lax.rsqrt(msb + EPS) * g * (1.0 - lam_init)


def _diff_prompt_kernel(q_ref, kv_ref, al_ref, sg_ref, o_ref, kt, vt, *, tq, tk, lam_init):
    qi = pl.program_id(1)

    @pl.when(qi == 0)
    def _():
        kv = kv_ref[0]
        kt[...] = _tile4(kv, 0)
        vt[...] = _tile4(kv, HEAD_W)

    nkv = _num_chunks(qi, tq, tk)
    causal = _causal_fn(qi, tq, tk)
    lam = _lambda_value(al_ref, lam_init)
    q = q_ref[0]
    scale = A_HEAD_DIM ** -0.5
    o = jnp.zeros((tq, GROUP_WIDTH), F32)
    for h in range(N_HEADS):
        q1 = jnp.where(_group_mask(GROUP_WIDTH, A_HEAD_DIM, 2 * h), q, 0).astype(BF16)
        q2 = jnp.where(_group_mask(GROUP_WIDTH, A_HEAD_DIM, 2 * h + 1), q, 0).astype(BF16)
        o1 = _flash(q1, kt, vt, nkv, tk, scale, causal)
        o2 = _flash(q2, kt, vt, nkv, tk, scale, causal)
        o = o + jnp.where(_group_mask(GROUP_WIDTH, HEAD_W, h), o1 - lam * o2, 0.0)
    o_ref[0] = _subln(o, sg_ref[...], lam_init, GROUP_WIDTH, 0).astype(BF16)


def _count_rows(keys_ref, nkv, pred):
    rows = keys_ref.shape[1]

    def body(c, acc):
        return acc + _lane_sum128(jnp.where(pred(keys_ref[c]), 1.0, 0.0))

    acc = lax.fori_loop(0, nkv, body, jnp.zeros((rows, LANE), F32))
    return jnp.sum(acc, axis=-1, keepdims=True)


def _select_topk(count_ge, count_lt, rewrite, rows, k, nbits):
    kf = float(k)

    def bit_body(i, t):
        cand = t | lax.shift_left(jnp.int32(1), 31 - i)
        return jnp.where(count_ge(cand ^ INT_MIN) >= kf, cand, t)

    t = lax.fori_loop(0, 32, bit_body, jnp.zeros((rows, 1), I32)) ^ INT_MIN
    rewrite(t)

    def pos_body(i, b):
        cand = b | lax.shift_left(jnp.int32(1), nbits - 1 - i)
        return jnp.where(count_lt(cand) < kf, cand, b)

    return lax.fori_loop(0, nbits, pos_body, jnp.zeros((rows, 1), I32))


def _dsa_prompt_kernel(q_ref, iq_ref, iw_ref, kv_ref, o_ref, kt, vt, it, keys, *, tq, tk, nsel, nbits):
    qi = pl.program_id(1)

    @pl.when(qi == 0)
    def _():
        kv = kv_ref[0]
        kt[...] = _tile4(kv, 0)
        vt[...] = _tile4(kv, HEAD_W)
        it[...] = _tile4(kv, 2 * HEAD_W)

    nkv = _num_chunks(qi, tq, tk)
    causal = _causal_fn(qi, tq, tk)
    iq = iq_ref[0]
    iw = iw_ref[0]
    wcols = [iw[:, HEAD_W + h:HEAD_W + h + 1] * (IDX_DIM ** -0.5 * N_HEADS ** -0.5) for h in range(N_HEADS)]
    iqh = [jnp.where(_group_mask(GROUP_WIDTH, HEAD_W, h), iq, 0).astype(BF16) for h in range(N_HEADS)]

    def score_body(c, carry):
        off = pl.multiple_of(c * tk, tk)
        ik = it[pl.ds(off, tk), :]
        sc = jnp.zeros((tq, tk), F32)
        for h in range(N_HEADS):
            sc = sc + jnp.maximum(_dot_nt(iqh[h], ik), 0.0) * wcols[h]
        keys[c] = jnp.where(causal(c), _sort_key(sc), INT_MIN)
        return carry

    lax.fori_loop(0, nkv, score_body, 0)

    lane = lax.broadcasted_iota(I32, (1, tk), 1)

    def rewrite(t):
        def body(c, carry):
            kc = keys[c]
            code = jnp.where(kc > t, -1, jnp.where(kc == t, c * tk + lane, BIG))
            keys[c] = jnp.where(causal(c), code, BIG)
            return carry
        lax.fori_loop(0, nkv, body, 0)

    bound = _select_topk(lambda cand: _count_rows(keys, nkv, lambda kc: kc >= cand),
                         lambda cand: _count_rows(keys, nkv, lambda kc: kc < cand),
                         rewrite, tq, nsel, nbits)

    q = q_ref[0]
    scale = HEAD_W ** -0.5
    mask_fn = lambda c: keys[c] <= bound
    o = jnp.zeros((tq, GROUP_WIDTH), F32)
    for h in range(N_HEADS):
        mh = _group_mask(GROUP_WIDTH, HEAD_W, h)
        oh = _flash(jnp.where(mh, q, 0).astype(BF16), kt, vt, nkv, tk, scale, mask_fn)
        o = o + jnp.where(mh, oh, 0.0)
    o_ref[0] = o.astype(BF16)


def _block_means(kf, first_key, nrows=LANE):
    n = kf.shape[0]
    j = lax.broadcasted_iota(I32, (nrows, n), 0)
    s = lax.broadcasted_iota(I32, (nrows, n), 1)
    sel = jnp.where((first_key + s) // MOBA_BLOCK == j, 1.0 / MOBA_BLOCK, 0.0).astype(BF16)
    hi, lo = _split_bf16(kf)
    return _dot(sel, hi) + _dot(sel, lo)


def _moba_prompt_kernel(q_ref, kv_ref, kf_ref, o_ref, kt, vt, kmh, kml, *, tq, nsel):
    qi = pl.program_id(1)
    tk = MOBA_BLOCK

    @pl.when(qi == 0)
    def _():
        kv = kv_ref[0]
        kt[...] = _tile4(kv, 0)
        vt[...] = _tile4(kv, HEAD_W)
        hi, lo = _split_bf16(_block_means(kf_ref[0], 0))
        kmh[...] = _tile4(hi, 0)
        kml[...] = _tile4(lo, 0)

    own = (qi * tq) // MOBA_BLOCK
    causal = _causal_fn(qi, tq, tk)
    q = q_ref[0]
    scale = HEAD_W ** -0.5
    lane = lax.broadcasted_iota(I32, (tq, LANE), 1)
    o = jnp.zeros((tq, GROUP_WIDTH), F32)
    for h in range(N_HEADS):
        mh = _group_mask(GROUP_WIDTH, HEAD_W, h)
        qh = jnp.where(mh, q, 0).astype(BF16)
        gate = _dot_nt(qh, kmh[...]) + _dot_nt(qh, kml[...])
        sel = jnp.where(_top_blocks(gate, own, nsel), 1.0, 0.0)

        def mask_fn(c, sel=sel):
            picked = jnp.max(jnp.where(lane == c, sel, 0.0), axis=-1, keepdims=True) > 0.5
            return (picked & (c < own)) | (causal(c) & (c >= own))

        oh = _flash(qh, kt, vt, own + 1, tk, scale, mask_fn)
        o = o + jnp.where(mh, oh, 0.0)
    o_ref[0] = o.astype(BF16)


def _mla_prompt_kernel(q_ref, kv_ref, wuv_ref, o_ref, *, tq, tk):
    qi = pl.program_id(1)
    nkv = _num_chunks(qi, tq, tk)
    causal = _causal_fn(qi, tq, tk)
    kv = kv_ref.at[0]
    scale = (D_NOPE + D_ROPE) ** -0.5
    o = jnp.zeros((tq, GROUP_WIDTH), F32)
    for h in range(N_HEADS):
        ol = _flash(q_ref[0, :, h * QX_W:(h + 1) * QX_W], kv, kv, nkv, tk, scale, causal)
        o = o + _dot(ol.astype(BF16), wuv_ref[h])
    o_ref[0] = o.astype(BF16)


def _prompt_call(kernel, name, nb, seq, tq, inputs, scratch):
    in_specs = []
    for arr, kind in inputs:
        if kind == 'q':
            in_specs.append(pl.BlockSpec((1, tq, arr.shape[-1]), lambda b, i: (b, i, 0)))
        elif kind == 'seq':
            in_specs.append(pl.BlockSpec((1, seq, arr.shape[-1]), lambda b, i: (b, 0, 0)))
        else:
            nd = arr.ndim
            in_specs.append(pl.BlockSpec(arr.shape, lambda b, i, nd=nd: (0,) * nd))
    return pl.pallas_call(
        kernel,
        grid=(nb, seq // tq),
        in_specs=in_specs,
        out_specs=pl.BlockSpec((1, tq, GROUP_WIDTH), lambda b, i: (b, i, 0)),
        out_shape=jax.ShapeDtypeStruct((nb, seq, GROUP_WIDTH), BF16),
        scratch_shapes=scratch,
        compiler_params=pltpu.CompilerParams(dimension_semantics=("arbitrary", "arbitrary"),
                                             vmem_limit_bytes=VMEM_LIMIT),
        name=name,
    )(*[a for a, _ in inputs])


def _prompt_attention(p, lw, nb, seq, lam_init):
    tq = min(256, seq)
    tk = min(512, seq)
    r3 = lambda a: a.reshape(nb, seq, a.shape[-1])
    tile = lambda: pltpu.VMEM((seq, GROUP_WIDTH), BF16)
    o_a = _prompt_call(
        functools.partial(_diff_prompt_kernel, tq=tq, tk=tk, lam_init=lam_init), "diff_prompt", nb, seq, tq,
        [(r3(p['aq']), 'q'), (r3(p['ka']), 'seq'), (lw['a_lam'], 'const'), (lw['sg4'], 'const')],
        [tile(), tile()])
    nsel = min(DSA_TOPK, seq // 4)
    o_b = _prompt_call(
        functools.partial(_dsa_prompt_kernel, tq=tq, tk=tk, nsel=nsel, nbits=max(1, (seq - 1).bit_length())),
        "dsa_prompt", nb, seq, tq,
        [(r3(p['bq']), 'q'), (r3(p['iq']), 'q'), (r3(p['iw']), 'q'), (r3(p['kb']), 'seq')],
        [tile(), tile(), tile(), pltpu.VMEM((seq // tk, tq, tk), I32)])
    assert seq % MOBA_BLOCK == 0 and MOBA_BLOCK % tq == 0 and seq // MOBA_BLOCK <= LANE
    o_c = _prompt_call(
        functools.partial(_moba_prompt_kernel, tq=tq, nsel=min(MOBA_TOPK, seq // MOBA_BLOCK)),
        "moba_prompt", nb, seq, tq,
        [(r3(p['cq']), 'q'), (r3(p['kc']), 'seq'), (r3(p['rc']), 'seq')],
        [tile(), tile(), pltpu.VMEM((LANE, GROUP_WIDTH), BF16), pltpu.VMEM((LANE, GROUP_WIDTH), BF16)])
    o_d = _prompt_call(
        functools.partial(_mla_prompt_kernel, tq=tq, tk=tk), "mla_prompt", nb, seq, tq,
        [(r3(p['qd']), 'q'), (r3(p['kd']), 'seq'), (lw['wuv'], 'const')], [])
    return [o.reshape(nb * seq, GROUP_WIDTH) for o in (o_a, o_b, o_c, o_d)]


TOK_PAD = 8
NEW_PAD = 128


def _fetch_pages(pt_ref, cache_ref, buf, sem, *, layer, npages, page):
    b = pl.program_id(0)
    slot = b % 2

    def copy(bb, sl, j):
        return pltpu.make_async_copy(cache_ref.at[layer, pt_ref[bb * npages + j]],
                                     buf.at[sl, pl.ds(pl.multiple_of(j * page, page), page), :], sem.at[sl])

    def issue(bb, sl):
        def body(j, carry):
            copy(bb, sl, j).start()
            return carry
        lax.fori_loop(0, npages, body, 0)

    @pl.when(b == 0)
    def _():
        issue(0, 0)

    @pl.when(b + 1 < pl.num_programs(0))
    def _():
        issue(b + 1, 1 - slot)

    def wait_body(j, carry):
        copy(b, slot, j).wait()
        return carry
    lax.fori_loop(0, npages, wait_body, 0)
    return slot


def _new_keys(new_ref):
    new = new_ref[0]
    return jnp.concatenate([new, jnp.zeros((NEW_PAD - TOK_PAD, new.shape[1]), F32)], axis=0).astype(BF16)


def _new_valid(rows, ntok):
    tok = lax.broadcasted_iota(I32, (rows, 1), 0) % TOK_PAD
    i = lax.broadcasted_iota(I32, (1, NEW_PAD), 1)
    return (i <= tok) & (i < ntok)


def _scores_to(s_ref, qx, buf, slot, new, npast, ck, scale):
    for c in range(npast // ck):
        kc = buf[slot, c * ck:(c + 1) * ck, :].astype(BF16)
        s_ref[:, c * ck:(c + 1) * ck] = _dot_nt(qx, kc) * scale
    s_ref[:, npast:npast + NEW_PAD] = _dot_nt(qx, new) * scale


def _softmax_rows(s_ref, ck):
    n = s_ref.shape[1]
    bounds = [(a, min(a + ck, n)) for a in range(0, n, ck)]
    m = jnp.full((s_ref.shape[0], 1), NEG, F32)
    for a, b in bounds:
        m = jnp.maximum(m, jnp.max(s_ref[:, a:b], axis=-1, keepdims=True))
    l = jnp.zeros((s_ref.shape[0], 1), F32)
    for a, b in bounds:
        p = jnp.exp(s_ref[:, a:b] - m)
        s_ref[:, a:b] = p
        l = l + jnp.sum(p, axis=-1, keepdims=True)
    return 1.0 / l


def _weighted_values(p_fn, buf, slot, new, npast, ck):
    acc = _dot(p_fn(npast, npast + NEW_PAD).astype(BF16), new)
    for c in range(npast // ck):
        vc = buf[slot, c * ck:(c + 1) * ck, :].astype(BF16)
        acc = acc + _dot(p_fn(c * ck, (c + 1) * ck).astype(BF16), vc)
    return acc


def _diff_sample_kernel(pt_ref, q_ref, new_ref, al_ref, sg_ref, cache_ref, o_ref, buf, sem, s_ref,
                        *, layer, npages, page, ck, ntok, lam_init):
    slot = _fetch_pages(pt_ref, cache_ref, buf, sem, layer=layer, npages=npages, page=page)
    npast = npages * page
    rows = 2 * N_HEADS * TOK_PAD
    half = rows // 2
    new = _new_keys(new_ref)
    _scores_to(s_ref, q_ref[0], buf, slot, new, npast, ck, A_HEAD_DIM ** -0.5)
    s_ref[:, npast:] = jnp.where(_new_valid(rows, ntok), s_ref[:, npast:], NEG)
    rl = _softmax_rows(s_ref, ck)
    lam = _lambda_value(al_ref, lam_init)
    r1, r2 = rl[:half], rl[half:] * lam
    o = _weighted_values(lambda a, b: s_ref[0:half, a:b] * r1 - s_ref[half:rows, a:b] * r2, buf, slot, new, npast, ck)
    o_ref[0] = _subln(o, sg_ref[...], lam_init, o.shape[1], HEAD_W)


def _dsa_sample_kernel(pt_ref, q_ref, iq_ref, iw_ref, new_ref, cache_ref, o_ref, buf, sem, s_ref, k_ref,
                       *, layer, npages, page, ck, ntok, nsel, nbits):
    slot = _fetch_pages(pt_ref, cache_ref, buf, sem, layer=layer, npages=npages, page=page)
    npast = npages * page
    n = npast + NEW_PAD
    rows = N_HEADS * TOK_PAD
    new = _new_keys(new_ref)
    iw = iw_ref[0]
    wcols = [iw[:, h:h + 1] * (IDX_DIM ** -0.5 * N_HEADS ** -0.5) for h in range(N_HEADS)]
    iqx = iq_ref[0]

    def index_scores(kc):
        si = _dot_nt(iqx, kc)
        sc = jnp.zeros((TOK_PAD, kc.shape[0]), F32)
        for h in range(N_HEADS):
            sc = sc + jnp.maximum(si[h * TOK_PAD:(h + 1) * TOK_PAD], 0.0) * wcols[h]
        return _sort_key(sc)

    for c in range(npast // ck):
        k_ref[:, c * ck:(c + 1) * ck] = index_scores(buf[slot, c * ck:(c + 1) * ck, :].astype(BF16))
    new_ok = _new_valid(TOK_PAD, ntok)
    k_ref[:, npast:] = jnp.where(new_ok, index_scores(new), INT_MIN)

    def count(pred):
        acc = jnp.zeros((TOK_PAD, LANE), F32)
        for j in range(n // LANE):
            acc = acc + jnp.where(pred(k_ref[:, j * LANE:(j + 1) * LANE]), 1.0, 0.0)
        return jnp.sum(acc, axis=-1, keepdims=True)

    lane = lax.broadcasted_iota(I32, (1, ck), 1)

    def rewrite(t):
        for c in range(npast // ck):
            kc = k_ref[:, c * ck:(c + 1) * ck]
            k_ref[:, c * ck:(c + 1) * ck] = jnp.where(kc > t, -1, jnp.where(kc == t, c * ck + lane, BIG))
        kc = k_ref[:, npast:]
        code = jnp.where(kc > t, -1, jnp.where(kc == t, npast + lane[:, :NEW_PAD], BIG))
        k_ref[:, npast:] = jnp.where(new_ok, code, BIG)

    bound = _select_topk(lambda cand: count(lambda kc: kc >= cand), lambda cand: count(lambda kc: kc < cand),
                         rewrite, TOK_PAD, nsel, nbits)

    _scores_to(s_ref, q_ref[0], buf, slot, new, npast, ck, HEAD_W ** -0.5)
    bounds = [(a, min(a + ck, n)) for a in range(0, n, ck)]
    for a, b in bounds:
        picked = k_ref[:, a:b] <= bound
        for h in range(N_HEADS):
            r = slice(h * TOK_PAD, (h + 1) * TOK_PAD)
            s_ref[r, a:b] = jnp.where(picked, s_ref[r, a:b], NEG)
    rl = _softmax_rows(s_ref, ck)
    o_ref[0] = _weighted_values(lambda a, b: s_ref[:, a:b] * rl, buf, slot, new, npast, ck)


def _moba_sample_kernel(pt_ref, q_ref, new_ref, cache_ref, o_ref, buf, sem, s_ref,
                        *, layer, npages, page, ck, ntok, nsel):
    slot = _fetch_pages(pt_ref, cache_ref, buf, sem, layer=layer, npages=npages, page=page)
    npast = npages * page
    rows = N_HEADS * TOK_PAD
    own = npast // MOBA_BLOCK
    new = _new_keys(new_ref)
    qx = q_ref[0]
    km = jnp.zeros((LANE, buf.shape[2]), F32)
    for c in range(npast // ck):
        km = km + _block_means(buf[slot, c * ck:(c + 1) * ck, :], c * ck)
    hi, lo = _split_bf16(km)
    gate = _dot_nt(qx, hi) + _dot_nt(qx, lo)
    sel = jnp.where(_top_blocks(gate, own, nsel), 1.0, 0.0).astype(BF16)
    _scores_to(s_ref, qx, buf, slot, new, npast, ck, HEAD_W ** -0.5)
    for c in range(npast // ck):
        j = lax.broadcasted_iota(I32, (LANE, ck), 0)
        s = lax.broadcasted_iota(I32, (LANE, ck), 1)
        expand = jnp.where((c * ck + s) // MOBA_BLOCK == j, 1.0, 0.0).astype(BF16)
        picked = _dot(sel, expand) > 0.5
        s_ref[:, c * ck:(c + 1) * ck] = jnp.where(picked, s_ref[:, c * ck:(c + 1) * ck], NEG)
    s_ref[:, npast:] = jnp.where(_new_valid(rows, ntok), s_ref[:, npast:], NEG)
    rl = _softmax_rows(s_ref, ck)
    o_ref[0] = _weighted_values(lambda a, b: s_ref[:, a:b] * rl, buf, slot, new, npast, ck)


def _mla_sample_kernel(pt_ref, q_ref, new_ref, wuv_ref, cache_ref, o_ref, buf, sem, s_ref,
                       *, layer, npages, page, ck, ntok):
    slot = _fetch_pages(pt_ref, cache_ref, buf, sem, layer=layer, npages=npages, page=page)
    npast = npages * page
    rows = N_HEADS * TOK_PAD
    new = _new_keys(new_ref)
    _scores_to(s_ref, q_ref[0], buf, slot, new, npast, ck, (D_NOPE + D_ROPE) ** -0.5)
    s_ref[:, npast:] = jnp.where(_new_valid(rows, ntok), s_ref[:, npast:], NEG)
    rl = _softmax_rows(s_ref, ck)
    ol = _weighted_values(lambda a, b: s_ref[:, a:b] * rl, buf, slot, new, npast, ck)
    o = jnp.zeros((TOK_PAD, GROUP_WIDTH), F32)
    for h in range(N_HEADS):
        o = o + _dot(ol[h * TOK_PAD:(h + 1) * TOK_PAD].astype(BF16), wuv_ref[h])
    o_ref[0] = o


def _sample_call(kernel, name, page_table, inputs, cache, out_rows, out_w, scratch):
    nb = page_table.shape[0]
    in_specs = []
    for arr, kind in inputs:
        if kind == 'b':
            in_specs.append(pl.BlockSpec((1,) + arr.shape[1:], lambda b, pt: (b, 0, 0)))
        else:
            nd = arr.ndim
            in_specs.append(pl.BlockSpec(arr.shape, lambda b, pt, nd=nd: (0,) * nd))
    in_specs.append(pl.BlockSpec(memory_space=pl.ANY))
    return pl.pallas_call(
        kernel,
        grid_spec=pltpu.PrefetchScalarGridSpec(
            num_scalar_prefetch=1,
            grid=(nb,),
            in_specs=in_specs,
            out_specs=pl.BlockSpec((1, out_rows, out_w), lambda b, pt: (b, 0, 0)),
            scratch_shapes=scratch,
        ),
        out_shape=jax.ShapeDtypeStruct((nb, out_rows, out_w), F32),
        compiler_params=pltpu.CompilerParams(dimension_semantics=("arbitrary",), vmem_limit_bytes=VMEM_LIMIT),
        name=name,
    )(page_table.reshape(-1), *[a for a, _ in inputs], cache)


def _head_rows(q, nb, ntok, lane_lo, width):
    w = q.shape[-1] // N_HEADS
    q = q.reshape(nb, ntok, N_HEADS, w).transpose(0, 2, 1, 3)
    q = jnp.pad(q, ((0, 0), (0, 0), (0, TOK_PAD - ntok), (lane_lo, width - lane_lo - w)))
    return q.reshape(nb, N_HEADS * TOK_PAD, width)


def _unhead_rows(o, nb, ntok, lane_lo):
    o = o.reshape(nb, N_HEADS, TOK_PAD, o.shape[-1])[:, :, :ntok, lane_lo:lane_lo + HEAD_W]
    return o.transpose(0, 2, 1, 3).reshape(nb * ntok, GROUP_WIDTH).astype(BF16)


def _sample_attention(s, lw, caches, page_table, layer, ntok, lam_init):
    nb, npages = page_table.shape
    page = caches[0].shape[2]
    npast = npages * page
    ck = min(1024, npast)
    n = npast + NEW_PAD
    assert npast % ck == 0 and npast % MOBA_BLOCK == 0 and ntok <= TOK_PAD and npast // MOBA_BLOCK < LANE
    new_rows = lambda r: jnp.pad(r.reshape(nb, ntok, r.shape[-1]), ((0, 0), (0, TOK_PAD - ntok), (0, 0)))
    common = dict(layer=layer, npages=npages, page=page, ck=ck, ntok=ntok)
    sem = pltpu.SemaphoreType.DMA((2,))
    rows = N_HEADS * TOK_PAD

    wa = CACHE_W[0]
    aq = s['aq'].reshape(nb, ntok, N_HEADS, 2, A_HEAD_DIM).transpose(0, 3, 2, 1, 4)
    aq = jnp.pad(aq, ((0, 0), (0, 0), (0, 0), (0, TOK_PAD - ntok), (0, 0)))
    aq = jnp.stack([jnp.pad(aq[:, m], ((0, 0), (0, 0), (0, 0), (m * A_HEAD_DIM, wa - (m + 1) * A_HEAD_DIM)))
                    for m in range(2)], axis=1).reshape(nb, 2 * rows, wa)
    o_a = _sample_call(
        functools.partial(_diff_sample_kernel, lam_init=lam_init, **common), "diff_sample", page_table,
        [(aq, 'b'), (new_rows(s['ra']), 'b'), (lw['a_lam'], 'const'), (lw['sg_pad'], 'const')], caches[0], rows, wa,
        [pltpu.VMEM((2, npast, wa), F32), sem, pltpu.VMEM((2 * rows, n), F32)])

    wb = CACHE_W[1]
    iw = jnp.pad(s['iw'][:, HEAD_W:HEAD_W + N_HEADS].reshape(nb, ntok, N_HEADS),
                 ((0, 0), (0, TOK_PAD - ntok), (0, LANE - N_HEADS)))
    nsel = min(DSA_TOPK, (npast + ntok) // 4)
    o_b = _sample_call(
        functools.partial(_dsa_sample_kernel, nsel=nsel, nbits=max(1, (n - 1).bit_length()), **common),
        "dsa_sample", page_table,
        [(_head_rows(s['bq'], nb, ntok, 0, wb), 'b'), (_head_rows(s['iq'], nb, ntok, 2 * HEAD_W, wb), 'b'),
         (iw, 'b'), (new_rows(s['rb']), 'b')], caches[1], rows, wb,
        [pltpu.VMEM((2, npast, wb), F32), sem, pltpu.VMEM((rows, n), F32), pltpu.VMEM((TOK_PAD, n), I32)])

    wc = CACHE_W[2]
    o_c = _sample_call(
        functools.partial(_moba_sample_kernel, nsel=min(MOBA_TOPK, npast // MOBA_BLOCK + 1), **common),
        "moba_sample", page_table,
        [(_head_rows(s['cq'], nb, ntok, 0, wc), 'b'), (new_rows(s['rc']), 'b')], caches[2], rows, wc,
        [pltpu.VMEM((2, npast, wc), F32), sem, pltpu.VMEM((rows, n), F32)])

    wd = CACHE_W[3]
    qd = s['qd'].reshape(nb * ntok, N_HEADS, QX_W)[:, :, :wd].reshape(nb * ntok, N_HEADS * wd)
    o_d = _sample_call(
        functools.partial(_mla_sample_kernel, **common), "mla_sample", page_table,
        [(_head_rows(qd, nb, ntok, 0, wd), 'b'), (new_rows(s['rd']), 'b'), (lw['wuv_s'], 'const')], caches[3],
        TOK_PAD, GROUP_WIDTH,
        [pltpu.VMEM((2, npast, wd), F32), sem, pltpu.VMEM((rows, n), F32)])

    return [_unhead_rows(o_a, nb, ntok, HEAD_W), _unhead_rows(o_b, nb, ntok, HEAD_W),
            _unhead_rows(o_c, nb, ntok, HEAD_W),
            o_d[:, :ntok].reshape(nb * ntok, GROUP_WIDTH).astype(BF16)]


def _prep_weights(attn_norm_g, w_in, a_lambda, a_subln_g, d_q_norm_g, d_kv_norm_g, d_w_uq, d_w_uk, d_w_uv,
                  w_out, ffn_norm_g, w_gate, w_up, w_down):
    depth = w_in.shape[0]
    src, swp, _, _ = _build_layout()
    w_pad = jnp.pad(w_in, ((0, 0), (0, 0), (0, 1)))
    w_cat = jnp.take(w_pad, jnp.asarray(np.concatenate([src, swp])), axis=2).astype(BF16)
    qsrc, qswp, _, _ = _build_q_layout()
    uq_pad = jnp.pad(d_w_uq, ((0, 0), (0, 0), (0, 1)))
    wuq = jnp.take(uq_pad, jnp.asarray(np.concatenate([qsrc, qswp])), axis=2).astype(BF16)
    eye_h = jnp.eye(N_HEADS, dtype=F32)
    nope = jnp.einsum('lchn,hg->lhngc', d_w_uk, eye_h)
    nope = jnp.pad(nope, ((0, 0),) * 4 + ((0, QX_W - KV_LORA),)).reshape(depth, N_HEADS * D_NOPE, N_HEADS * QX_W)
    rope = np.zeros((N_HEADS, D_ROPE, N_HEADS, QX_W), np.float32)
    for h in range(N_HEADS):
        rope[h, np.arange(D_ROPE), h, KV_LORA + np.arange(D_ROPE)] = 1.0
    rope = jnp.broadcast_to(jnp.asarray(rope.reshape(N_HEADS * D_ROPE, N_HEADS * QX_W))[None],
                            (depth, N_HEADS * D_ROPE, N_HEADS * QX_W))
    bd = jnp.concatenate([nope, rope], axis=1).astype(BF16)
    wuv = jnp.einsum('lchv,hg->lhcgv', d_w_uv, eye_h).reshape(depth, N_HEADS, KV_LORA, GROUP_WIDTH)
    wuv_s = jnp.pad(wuv, ((0, 0), (0, 0), (0, CACHE_W[3] - KV_LORA), (0, 0))).astype(BF16)
    wuv = jnp.pad(wuv, ((0, 0), (0, 0), (0, QX_W - KV_LORA), (0, 0))).astype(BF16)
    layers = []
    for l in range(depth):
        layers.append(dict(
            attn_g=attn_norm_g[l][None], w_cat=w_cat[l], gq=d_q_norm_g[l][None], gkv=d_kv_norm_g[l][None],
            wuq=wuq[l], bd=bd[l], wuv=wuv[l], wuv_s=wuv_s[l], a_lam=a_lambda[l],
            sg4=jnp.tile(a_subln_g[l], N_HEADS)[None],
            sg_pad=jnp.pad(a_subln_g[l], (HEAD_W, 0))[None],
            wo=w_out[l].astype(BF16), ffn_g=ffn_norm_g[l][None],
            wg=w_gate[l].astype(BF16), wu=w_up[l].astype(BF16), wd=w_down[l].astype(BF16)))
    return layers


def kernel(x_prompt, x_sample, cache_diff_kv, cache_dsa_kv, cache_moba_kv, cache_mla_latent, page_table,
           attn_norm_g, w_in, a_lambda, a_subln_g, d_q_norm_g, d_kv_norm_g, d_w_uq, d_w_uk, d_w_uv,
           w_out, ffn_norm_g, w_gate, w_up, w_down, final_norm_g):
    nb, seq, d = x_prompt.shape
    db, ntok, _ = x_sample.shape
    depth = w_in.shape[0]
    caches = (cache_diff_kv, cache_dsa_kv, cache_moba_kv, cache_mla_latent)
    past_len = page_table.shape[1] * cache_diff_kv.shape[2]
    layers = _prep_weights(attn_norm_g, w_in, a_lambda, a_subln_g, d_q_norm_g, d_kv_norm_g, d_w_uq, d_w_uk,
                           d_w_uv, w_out, ffn_norm_g, w_gate, w_up, w_down)
    _, _, fidx, sgn = _build_layout()
    _, _, qfidx, qsgn = _build_q_layout()
    pos_p = jnp.arange(seq, dtype=jnp.int32)
    pos_s = jnp.tile(past_len + jnp.arange(ntok, dtype=jnp.int32), db)
    tabs_p = _rope_tables(pos_p, fidx, sgn) + _rope_tables(pos_p, qfidx, qsgn)
    tabs_s = _rope_tables(pos_s, fidx, sgn) + _rope_tables(pos_s, qfidx, qsgn)
    tabs_p = (tabs_p[0], tabs_p[1], tabs_p[2], tabs_p[3])
    tabs_s = (tabs_s[0], tabs_s[1], tabs_s[2], tabs_s[3])
    ntab_p = seq // _token_tile(nb * seq) if seq % _token_tile(nb * seq) == 0 else None
    assert ntab_p is not None
    ntab_s = (db * ntok) // _token_tile(db * ntok)

    hp = x_prompt.reshape(nb * seq, d)
    hs = x_sample.reshape(db * ntok, d)
    rows_p = {k: [] for k in ('ra', 'rb', 'rc', 'rd')}
    rows_s = {k: [] for k in ('ra', 'rb', 'rc', 'rd')}
    y_p = y_s = None
    for l in range(depth):
        lam_init = 0.8 - 0.6 * math.exp(-0.3 * l)
        lw = layers[l]
        fin = final_norm_g[None] if l == depth - 1 else None
        p = _inproj(hp, tabs_p, lw, ntab_p)
        o_p = _prompt_attention(p, lw, nb, seq, lam_init)
        hp, y_p = _ffn(hp, o_p, lw, fin)
        s = _inproj(hs, tabs_s, lw, ntab_s)
        o_s = _sample_attention(s, lw, caches, page_table, l, ntok, lam_init)
        hs, y_s = _ffn(hs, o_s, lw, fin)
        for k in rows_p:
            rows_p[k].append(p[k].reshape(nb, seq, -1))
            rows_s[k].append(s[k].reshape(db, ntok, -1))
    stack = lambda xs: jnp.stack(xs)
    return (y_p.reshape(nb, seq, d), y_s.reshape(db, ntok, d),
            stack(rows_p['ra']), stack(rows_p['rb']), stack(rows_p['rc']), stack(rows_p['rd']),
            stack(rows_s['ra']), stack(rows_s['rb']), stack(rows_s['rc']), stack(rows_s['rd']))
```

```python
import functools
import math

import numpy as np
import jax
import jax.numpy as jnp
from jax import lax
from jax.experimental import pallas as pl
from jax.experimental.pallas import tpu as pltpu

F32, BF16, I32 = jnp.float32, jnp.bfloat16, jnp.int32

N_HEADS = 4
GROUP_WIDTH = 256
HEAD_W = 64
A_HEAD_DIM = 32
IDX_DIM = 64
DSA_TOPK = 256
MOBA_BLOCK = 256
MOBA_TOPK = 3
D_NOPE, D_ROPE, KV_LORA, Q_LORA = 64, 32, 128, 256
ROPE_THETA = 500000.0
EPS = 1e-6
CACHE_W = (128, 192, 128, 160)

_NAMES = ('aq', 'ak', 'av', 'bq', 'bk', 'bv', 'iq', 'ik', 'iw', 'cq', 'ck', 'cv', 'dcq', 'dckv', 'dkr')
_SIZES = (256, 64, 64, 256, 64, 64, 256, 64, 4, 256, 64, 64, 256, 128, 32)
_ORIG = dict(zip(_NAMES, (int(v) for v in np.cumsum((0,) + _SIZES[:-1]))))
IN_WIDTH = sum(_SIZES)
_NEW = {'aq': 0, 'bq': 256, 'iq': 512, 'cq': 768, 'dcq': 1024, 'ak': 1280, 'av': 1344, 'ck': 1408, 'cv': 1472,
        'bk': 1536, 'bv': 1600, 'ik': 1664, 'iw': 1728, 'dckv': 1792, 'dkr': 1920}
WP = 2048
_ROT = {'aq': (32, 8), 'ak': (32, 8), 'bq': (64, 16), 'bk': (64, 16), 'iq': (64, 16), 'ik': (64, 16),
        'cq': (64, 16), 'ck': (64, 16), 'dkr': (32, 32)}
_ROT_DIMS = (8, 16, 32)
_FBASE = {8: 0, 16: 4, 32: 12}
QD_W = N_HEADS * (D_NOPE + D_ROPE)
QX_W = 256

LANE = 128
INT_MIN = -2 ** 31
NEG = -1e30
BIG = 2 ** 30
VMEM_LIMIT = 56 * 1024 * 1024


def _build_layout():
    src = np.full(WP, IN_WIDTH, np.int32)
    swp = np.full(WP, IN_WIDTH, np.int32)
    fidx = np.zeros(WP, np.int32)
    sgn = np.zeros(WP, np.float32)
    for name, size in zip(_NAMES, _SIZES):
        for j in range(size):
            c, o = _NEW[name] + j, _ORIG[name] + j
            src[c] = o
            if name in _ROT:
                gw, r = _ROT[name]
                d, half = j % gw, r // 2
                if d < r:
                    first = d < half
                    swp[c] = o + half if first else o - half
                    fidx[c] = _FBASE[r] + d % half
                    sgn[c] = -1.0 if first else 1.0
    return src, swp, fidx, sgn


def _build_q_layout():
    per = D_NOPE + D_ROPE
    src = np.zeros(QD_W, np.int32)
    swp = np.full(QD_W, QD_W, np.int32)
    fidx = np.zeros(QD_W, np.int32)
    sgn = np.zeros(QD_W, np.float32)
    half = D_ROPE // 2
    for h in range(N_HEADS):
        for n in range(D_NOPE):
            src[h * D_NOPE + n] = h * per + n
        for r in range(D_ROPE):
            c, o = N_HEADS * D_NOPE + h * D_ROPE + r, h * per + D_NOPE + r
            src[c] = o
            first = r < half
            swp[c] = o + half if first else o - half
            fidx[c] = _FBASE[D_ROPE] + r % half
            sgn[c] = -1.0 if first else 1.0
    return src, swp, fidx, sgn


def _rope_tables(pos, fidx, sgn):
    inv = jnp.concatenate([ROPE_THETA ** (-(jnp.arange(r // 2, dtype=F32) * 2.0 / r)) for r in _ROT_DIMS])
    ang = pos.astype(F32)[:, None] * inv[None, :]
    cos_f, sin_f = jnp.cos(ang), jnp.sin(ang)
    rot = jnp.asarray(sgn != 0)
    cos = jnp.where(rot[None, :], cos_f[:, fidx], 1.0)
    sin = sin_f[:, fidx] * jnp.asarray(sgn)[None, :]
    return cos, sin


def _dot(a, b):
    return jnp.dot(a, b, preferred_element_type=F32)


def _dot_nt(a, b):
    return lax.dot_general(a, b, (((1,), (1,)), ((), ())), preferred_element_type=F32)


def _group_mask(width, group, idx):
    lane = lax.broadcasted_iota(I32, (1, width), 1)
    return (lane // group) == idx


def _tile4(x, base):
    w = x.shape[1]
    j = lax.broadcasted_iota(I32, (w, GROUP_WIDTH), 0)
    c = lax.broadcasted_iota(I32, (w, GROUP_WIDTH), 1)
    rep = jnp.where(j == base + (c % HEAD_W), 1.0, 0.0).astype(BF16)
    return _dot(x, rep).astype(BF16)


def _split_bf16(x):
    hi = x.astype(BF16)
    lo = (x - hi.astype(F32)).astype(BF16)
    return hi, lo


def _rms(x, g):
    return x * lax.rsqrt(jnp.mean(x * x, axis=-1, keepdims=True) + EPS) * g


def _sort_key(x):
    b = lax.bitcast_convert_type(x, I32)
    return b ^ ((b >> 31) & 0x7FFFFFFF)


def _lane_sum128(m):
    acc = m[:, 0:LANE]
    for j in range(1, m.shape[1] // LANE):
        acc = acc + m[:, j * LANE:(j + 1) * LANE]
    return acc


def _by_lane(cols, lane_masks):
    out = cols[-1]
    for col, mask in zip(cols[-2::-1], lane_masks[-2::-1]):
        out = jnp.where(mask, col, out)
    return out


def _flash(qhs, k_ref, v_loads, wv, nkv, tk, scale, mask_fns, groups=None, lane_masks=None):
    tq = qhs[0].shape[0]
    n = len(qhs)
    if not isinstance(mask_fns, (list, tuple)):
        mask_fns = [mask_fns] * n
    if groups is None:
        groups = [[i] for i in range(n)]

    def spread(cols, g):
        if len(g) == 1:
            return cols[g[0]]
        return _by_lane([cols[i] for i in g], [lane_masks[i] for i in g])

    def body(c, carry):
        ms, ls, accs = carry
        off = pl.multiple_of(c * tk, tk)
        k = k_ref[pl.ds(off, tk), :]
        masks = {}
        new_m, new_l, alphas, pvs = [], [], [], []
        for i in range(n):
            fn = mask_fns[i]
            if fn not in masks:
                masks[fn] = fn(c)
            s = jnp.where(masks[fn], _dot_nt(qhs[i], k) * scale, NEG)
            m_new = jnp.maximum(ms[i], jnp.max(s, axis=-1, keepdims=True))
            alpha = jnp.exp(ms[i] - m_new)
            p = jnp.exp(s - m_new)
            new_m.append(m_new)
            new_l.append(alpha * ls[i] + jnp.sum(p, axis=-1, keepdims=True))
            alphas.append(alpha)
            pvs.append(_dot(p.astype(BF16), v_loads[i](off)))
        new_acc = []
        for g, acc in zip(groups, accs):
            pv = pvs[g[0]]
            for i in g[1:]:
                pv = pv + pvs[i]
            new_acc.append(spread(alphas, g) * acc + pv)
        return tuple(new_m), tuple(new_l), tuple(new_acc)

    init = (tuple(jnp.full((tq, 1), NEG, F32) for _ in range(n)), tuple(jnp.zeros((tq, 1), F32) for _ in range(n)),
            tuple(jnp.zeros((tq, wv), F32) for _ in groups))
    _, ls, accs = lax.fori_loop(0, nkv, body, init)
    return [acc / spread(ls, g) for g, acc in zip(groups, accs)]


def _causal_fn(qi, tq, tk):
    qpos = qi * tq + lax.broadcasted_iota(I32, (tq, 1), 0)
    lane = lax.broadcasted_iota(I32, (1, tk), 1)
    return lambda c: (c * tk + lane) <= qpos


def _top_blocks(g, own, nsel):
    lane = lax.broadcasted_iota(I32, g.shape, 1)
    lane_f = lane.astype(F32)
    past = lane < own
    g = jnp.where(past, g, -jnp.inf)
    sel = jnp.zeros(g.shape, jnp.bool_)
    for _ in range(nsel):
        gm = jnp.where(sel, -jnp.inf, g)
        top = jnp.max(gm, axis=-1, keepdims=True)
        first = jnp.min(jnp.where(gm == top, lane_f, float(LANE)), axis=-1, keepdims=True)
        sel = sel | (lane_f == first)
    return sel & past


def _inproj_kernel(x_ref, g_ref, w_ref, cos_ref, sin_ref, gq_ref, gkv_ref, wuq_ref, cosq_ref, sinq_ref, bd_ref,
                   aq_o, bq_o, iq_o, cq_o, qd_o, iw_o, ra_o, rb_o, rc_o, rd_o, ka_o, kb_o, kc_o, kd_o):
    h = _rms(x_ref[...], g_ref[...]).astype(BF16)
    y2 = _dot(h, w_ref[...])
    y = y2[:, :WP] * cos_ref[...] + y2[:, WP:] * sin_ref[...]
    aq_o[...] = y[:, 0:256].astype(BF16)
    bq_o[...] = y[:, 256:512].astype(BF16)
    iq_o[...] = y[:, 512:768].astype(BF16)
    cq_o[...] = y[:, 768:1024].astype(BF16)
    dn = _rms(y[:, 1024:1280], gq_ref[...]).astype(BF16)
    q2 = _dot(dn, wuq_ref[...])
    qd = q2[:, :QD_W] * cosq_ref[...] + q2[:, QD_W:] * sinq_ref[...]
    qd_o[...] = _dot(qd.astype(BF16), bd_ref[...]).astype(BF16)
    ra = y[:, 1280:1408]
    ra_o[...] = ra
    ka_o[...] = ra.astype(BF16)
    rc = y[:, 1408:1536]
    rc_o[...] = rc
    kc_o[...] = rc.astype(BF16)
    rb = y[:, 1536:1792]
    rb_o[...] = rb[:, :192]
    kb_o[...] = rb.astype(BF16)
    iw_o[...] = rb[:, 128:256]
    rd = y[:, 1792:2048]
    ckv = _rms(rd[:, :KV_LORA], gkv_ref[...])
    rd_o[:, 0:KV_LORA] = ckv
    rd_o[:, KV_LORA:KV_LORA + D_ROPE] = rd[:, KV_LORA:KV_LORA + D_ROPE]
    kd_o[:, 0:KV_LORA] = ckv.astype(BF16)
    kd_o[:, KV_LORA:QX_W] = rd[:, KV_LORA:QX_W].astype(BF16)


_K1_OUT = (('aq', 256, BF16), ('bq', 256, BF16), ('iq', 256, BF16), ('cq', 256, BF16), ('qd', N_HEADS * QX_W, BF16),
           ('iw', 128, F32), ('ra', 128, F32), ('rb', 192, F32), ('rc', 128, F32), ('rd', 160, F32),
           ('ka', 128, BF16), ('kb', 256, BF16), ('kc', 128, BF16), ('kd', 256, BF16))


def _token_tile(n):
    return 256 if n % 256 == 0 else n


def _inproj(x2d, tabs, lw, ntab):
    n, d = x2d.shape
    tm = _token_tile(n)
    cos, sin, cosq, sinq = tabs
    row = lambda i: (i, 0)
    tab = lambda i: (i % ntab, 0)
    const = lambda i: (0, 0)
    one = pl.Buffered(1)
    in_specs = [
        pl.BlockSpec((tm, d), row),
        pl.BlockSpec((1, d), const),
        pl.BlockSpec((d, 2 * WP), const, pipeline_mode=one),
        pl.BlockSpec((tm, WP), tab),
        pl.BlockSpec((tm, WP), tab),
        pl.BlockSpec((1, Q_LORA), const),
        pl.BlockSpec((1, KV_LORA), const),
        pl.BlockSpec((Q_LORA, 2 * QD_W), const, pipeline_mode=one),
        pl.BlockSpec((tm, QD_W), tab),
        pl.BlockSpec((tm, QD_W), tab),
        pl.BlockSpec((QD_W, N_HEADS * QX_W), const, pipeline_mode=one),
    ]
    outs = pl.pallas_call(
        _inproj_kernel,
        grid=(n // tm,),
        in_specs=in_specs,
        out_specs=[pl.BlockSpec((tm, w), row) for _, w, _ in _K1_OUT],
        out_shape=[jax.ShapeDtypeStruct((n, w), dt) for _, w, dt in _K1_OUT],
        compiler_params=pltpu.CompilerParams(dimension_semantics=("arbitrary",), vmem_limit_bytes=VMEM_LIMIT),
        name="inproj",
    )(x2d, lw['attn_g'], lw['w_cat'], cos, sin, lw['gq'], lw['gkv'], lw['wuq'], cosq, sinq, lw['bd'])
    return {name: o for (name, _, _), o in zip(_K1_OUT, outs)}


def _ffn_kernel(x_ref, oa_ref, ob_ref, oc_ref, od_ref, wo_ref, g_ref, wg_ref, wu_ref, wd_ref, *rest, final):
    if final:
        gf_ref, x_out, y_out = rest
    else:
        (x_out,) = rest
    x = x_ref[...]
    for i, o_ref in enumerate((oa_ref, ob_ref, oc_ref, od_ref)):
        x = x + _dot(o_ref[...], wo_ref[i * GROUP_WIDTH:(i + 1) * GROUP_WIDTH, :])
    h = _rms(x, g_ref[...]).astype(BF16)
    gate = _dot(h, wg_ref[...])
    up = _dot(h, wu_ref[...])
    act = gate * (1.0 / (1.0 + jnp.exp(-gate))) * up
    x = x + _dot(act.astype(BF16), wd_ref[...])
    x_out[...] = x
    if final:
        y_out[...] = _rms(x, gf_ref[...])


def _ffn(x2d, o_parts, lw, final_g):
    n, d = x2d.shape
    tm = _token_tile(n)
    dff = lw['wg'].shape[1]
    final = final_g is not None
    row = lambda i: (i, 0)
    const = lambda i: (0, 0)
    one = pl.Buffered(1)
    in_specs = [pl.BlockSpec((tm, d), row)] + [pl.BlockSpec((tm, GROUP_WIDTH), row)] * 4 + [
        pl.BlockSpec((d, d), const, pipeline_mode=one),
        pl.BlockSpec((1, d), const),
        pl.BlockSpec((d, dff), const, pipeline_mode=one),
        pl.BlockSpec((d, dff), const, pipeline_mode=one),
        pl.BlockSpec((dff, d), const, pipeline_mode=one),
    ]
    args = [x2d, *o_parts, lw['wo'], lw['ffn_g'], lw['wg'], lw['wu'], lw['wd']]
    out_specs = [pl.BlockSpec((tm, d), row)]
    out_shape = [jax.ShapeDtypeStruct((n, d), F32)]
    if final:
        in_specs.append(pl.BlockSpec((1, d), const))
        args.append(final_g)
        out_specs.append(pl.BlockSpec((tm, d), row))
        out_shape.append(jax.ShapeDtypeStruct((n, d), F32))
    outs = pl.pallas_call(
        functools.partial(_ffn_kernel, final=final),
        grid=(n // tm,),
        in_specs=in_specs,
        out_specs=out_specs,
        out_shape=out_shape,
        compiler_params=pltpu.CompilerParams(dimension_semantics=("arbitrary",), vmem_limit_bytes=VMEM_LIMIT),
        name="outproj_ffn",
    )(*args)
    return outs if final else (outs[0], None)


def _num_chunks(qi, tq, tk):
    return ((qi + 1) * tq + tk - 1) // tk


def _lambda_value(al_ref, lam_init):
    al = al_ref[...]
    s1 = jnp.sum(al[0:1, :] * al[1:2, :], axis=-1, keepdims=True)
    s2 = jnp.sum(al[2:3, :] * al[3:4, :], axis=-1, keepdims=True)
    return jnp.exp(s1) - jnp.exp(s2) + lam_init


def _subln(o, g, lam_init, width, first_lane):
    msb = jnp.zeros(o.shape, F32)
    for hh in range(first_lane // HEAD_W, width // HEAD_W):
        mh = _group_mask(width, HEAD_W, hh)
        ms = jnp.sum(jnp.where(mh, o * o, 0.0), axis=-1, keepdims=True) * (1.0 / HEAD_W)
        msb = msb + jnp.where(mh, ms, 0.0)
    return o * lax.rsqrt(msb + EPS) * g * (1.0 - lam_init)


def _store_head_values(vt, kv, base):
    tile = _tile4(kv, base)
    for h in range(N_HEADS):
        vt[h] = jnp.where(_group_mask(GROUP_WIDTH, HEAD_W, h), tile, 0).astype(BF16)


def _head_value_loads(vt, tk):
    return [lambda off, h=h: vt[h, pl.ds(off, tk), :] for h in range(N_HEADS)]


def _diff_prompt_kernel(q_ref, kv_ref, al_ref, sg_ref, o_ref, kt, vt, *, tq, tk, lam_init):
    qi = pl.program_id(1)

    @pl.when(qi == 0)
    def _():
        kv = kv_ref[0]
        kt[...] = _tile4(kv, 0)
        vt[...] = _tile4(kv, HEAD_W)

    nkv = _num_chunks(qi, tq, tk)
    causal = _causal_fn(qi, tq, tk)
    lam = _lambda_value(al_ref, lam_init)
    q = q_ref[0]
    scale = A_HEAD_DIM ** -0.5
    values = lambda off: vt[pl.ds(off, tk), :]
    o = jnp.zeros((tq, GROUP_WIDTH), F32)
    for h0 in range(0, N_HEADS, 2):
        qg = [jnp.where(_group_mask(GROUP_WIDTH, A_HEAD_DIM, 2 * h0 + g), q, 0).astype(BF16) for g in range(4)]
        og = _flash(qg, kt, [values] * 4, GROUP_WIDTH, nkv, tk, scale, causal)
        for j in range(2):
            o = o + jnp.where(_group_mask(GROUP_WIDTH, HEAD_W, h0 + j), og[2 * j] - lam * og[2 * j + 1], 0.0)
    o_ref[0] = _subln(o, sg_ref[...], lam_init, GROUP_WIDTH, 0).astype(BF16)


def _count_rows(keys_ref, nkv, preds):
    rows = keys_ref.shape[1]

    def body(c, accs):
        kc = keys_ref[c]
        return tuple(acc + _lane_sum128(jnp.where(p(kc), 1.0, 0.0)) for acc, p in zip(accs, preds))

    accs = lax.fori_loop(0, nkv, body, tuple(jnp.zeros((rows, LANE), F32) for _ in preds))
    return [jnp.sum(acc, axis=-1, keepdims=True) for acc in accs]


def _select_topk(count_ge, count_lt, rewrite, rows, k, nbits, bits_per_step):
    kf = float(k)
    ncand = (1 << bits_per_step) - 1

    def bit_body(i, t):
        shift = 32 - bits_per_step * (i + 1)
        cands = [t | lax.shift_left(jnp.int32(j), shift) for j in range(1, ncand + 1)]
        counts = count_ge([c ^ INT_MIN for c in cands])
        for c, n in zip(cands, counts):
            t = jnp.where(n >= kf, c, t)
        return t

    t = lax.fori_loop(0, 32 // bits_per_step, bit_body, jnp.zeros((rows, 1), I32)) ^ INT_MIN
    rewrite(t)

    def pos_body(i, b):
        cand = b | lax.shift_left(jnp.int32(1), nbits - 1 - i)
        return jnp.where(count_lt([cand])[0] < kf, cand, b)

    def search():
        return lax.fori_loop(0, nbits, pos_body, jnp.zeros((rows, 1), I32))

    tied = jnp.max(count_lt([jnp.full((rows, 1), BIG, I32)])[0]) > kf
    return lax.cond(tied, search, lambda: jnp.full((rows, 1), BIG - 1, I32))


def _dsa_prompt_kernel(q_ref, iq_ref, iw_ref, kv_ref, o_ref, kt, vt, it, keys, *, tq, tk, nsel, nbits):
    qi = pl.program_id(1)

    @pl.when(qi == 0)
    def _():
        kv = kv_ref[0]
        kt[...] = _tile4(kv, 0)
        _store_head_values(vt, kv, HEAD_W)
        it[...] = _tile4(kv, 2 * HEAD_W)

    nkv = _num_chunks(qi, tq, tk)
    causal = _causal_fn(qi, tq, tk)
    iq = iq_ref[0]
    iw = iw_ref[0]
    wcols = [iw[:, HEAD_W + h:HEAD_W + h + 1] * (IDX_DIM ** -0.5 * N_HEADS ** -0.5) for h in range(N_HEADS)]
    iqh = [jnp.where(_group_mask(GROUP_WIDTH, HEAD_W, h), iq, 0).astype(BF16) for h in range(N_HEADS)]

    def score_body(c, carry):
        off = pl.multiple_of(c * tk, tk)
        ik = it[pl.ds(off, tk), :]
        sc = jnp.zeros((tq, tk), F32)
        for h in range(N_HEADS):
            sc = sc + jnp.maximum(_dot_nt(iqh[h], ik), 0.0) * wcols[h]
        keys[c] = jnp.where(causal(c), _sort_key(sc), INT_MIN)
        return carry

    lax.fori_loop(0, nkv, score_body, 0)

    lane = lax.broadcasted_iota(I32, (1, tk), 1)

    def rewrite(t):
        def body(c, carry):
            kc = keys[c]
            code = jnp.where(kc > t, -1, jnp.where(kc == t, c * tk + lane, BIG))
            keys[c] = jnp.where(causal(c), code, BIG)
            return carry
        lax.fori_loop(0, nkv, body, 0)

    bound = _select_topk(lambda cs: _count_rows(keys, nkv, [lambda kc, c=c: kc >= c for c in cs]),
                         lambda cs: _count_rows(keys, nkv, [lambda kc, c=c: kc < c for c in cs]),
                         rewrite, tq, nsel, nbits, 1)

    q = q_ref[0]
    scale = HEAD_W ** -0.5
    mask_fn = lambda c: keys[c] <= bound
    mhs = [_group_mask(GROUP_WIDTH, HEAD_W, h) for h in range(N_HEADS)]
    (o,) = _flash([jnp.where(mh, q, 0).astype(BF16) for mh in mhs], kt, _head_value_loads(vt, tk), GROUP_WIDTH,
                  nkv, tk, scale, mask_fn, groups=[list(range(N_HEADS))], lane_masks=mhs)
    o_ref[0] = o.astype(BF16)


def _block_means(kf, first_key, nrows=LANE):
    n = kf.shape[0]
    j = lax.broadcasted_iota(I32, (nrows, n), 0)
    s = lax.broadcasted_iota(I32, (nrows, n), 1)
    sel = jnp.where((first_key + s) // MOBA_BLOCK == j, 1.0 / MOBA_BLOCK, 0.0).astype(BF16)
    hi, lo = _split_bf16(kf)
    return _dot(sel, hi) + _dot(sel, lo)


def _moba_prompt_kernel(q_ref, kv_ref, kf_ref, o_ref, kt, vt, kmh, kml, *, tq, nsel):
    qi = pl.program_id(1)
    tk = MOBA_BLOCK

    @pl.when(qi == 0)
    def _():
        kv = kv_ref[0]
        kt[...] = _tile4(kv, 0)
        _store_head_values(vt, kv, HEAD_W)
        hi, lo = _split_bf16(_block_means(kf_ref[0], 0))
        kmh[...] = _tile4(hi, 0)
        kml[...] = _tile4(lo, 0)

    own = (qi * tq) // MOBA_BLOCK
    causal = _causal_fn(qi, tq, tk)
    q = q_ref[0]
    scale = HEAD_W ** -0.5
    lane = lax.broadcasted_iota(I32, (tq, LANE), 1)
    mhs = [_group_mask(GROUP_WIDTH, HEAD_W, h) for h in range(N_HEADS)]
    qhs = [jnp.where(mh, q, 0).astype(BF16) for mh in mhs]
    mask_fns = []
    for qh in qhs:
        gate = _dot_nt(qh, kmh[...]) + _dot_nt(qh, kml[...])
        sel = jnp.where(_top_blocks(gate, own, nsel), 1.0, 0.0)

        def mask_fn(c, sel=sel):
            picked = jnp.max(jnp.where(lane == c, sel, 0.0), axis=-1, keepdims=True) > 0.5
            return (picked & (c < own)) | (causal(c) & (c >= own))

        mask_fns.append(mask_fn)
    (o,) = _flash(qhs, kt, _head_value_loads(vt, tk), GROUP_WIDTH, own + 1, tk, scale, mask_fns,
                  groups=[list(range(N_HEADS))], lane_masks=mhs)
    o_ref[0] = o.astype(BF16)


def _mla_prompt_kernel(q_ref, kv_ref, wuv_ref, o_ref, *, tq, tk):
    qi = pl.program_id(1)
    nkv = _num_chunks(qi, tq, tk)
    causal = _causal_fn(qi, tq, tk)
    kv = kv_ref.at[0]
    scale = (D_NOPE + D_ROPE) ** -0.5
    values = lambda off: kv_ref[0, pl.ds(off, tk), :]
    ols = _flash([q_ref[0, :, h * QX_W:(h + 1) * QX_W] for h in range(N_HEADS)], kv, [values] * N_HEADS, QX_W,
                 nkv, tk, scale, causal)
    o = jnp.zeros((tq, GROUP_WIDTH), F32)
    for h in range(N_HEADS):
        o = o + _dot(ols[h].astype(BF16), wuv_ref[h])
    o_ref[0] = o.astype(BF16)


def _prompt_call(kernel, name, nb, seq, tq, inputs, scratch):
    in_specs = []
    for arr, kind in inputs:
        if kind == 'q':
            in_specs.append(pl.BlockSpec((1, tq, arr.shape[-1]), lambda b, i: (b, i, 0)))
        elif kind == 'seq':
            in_specs.append(pl.BlockSpec((1, seq, arr.shape[-1]), lambda b, i: (b, 0, 0)))
        else:
            nd = arr.ndim
            in_specs.append(pl.BlockSpec(arr.shape, lambda b, i, nd=nd: (0,) * nd))
    return pl.pallas_call(
        kernel,
        grid=(nb, seq // tq),
        in_specs=in_specs,
        out_specs=pl.BlockSpec((1, tq, GROUP_WIDTH), lambda b, i: (b, i, 0)),
        out_shape=jax.ShapeDtypeStruct((nb, seq, GROUP_WIDTH), BF16),
        scratch_shapes=scratch,
        compiler_params=pltpu.CompilerParams(dimension_semantics=("arbitrary", "arbitrary"),
                                             vmem_limit_bytes=VMEM_LIMIT),
        name=name,
    )(*[a for a, _ in inputs])


def _prompt_attention(p, lw, nb, seq, lam_init):
    tq = min(256, seq)
    tk = min(512, seq)
    r3 = lambda a: a.reshape(nb, seq, a.shape[-1])
    tile = lambda: pltpu.VMEM((seq, GROUP_WIDTH), BF16)
    head_tiles = lambda: pltpu.VMEM((N_HEADS, seq, GROUP_WIDTH), BF16)
    o_a = _prompt_call(
        functools.partial(_diff_prompt_kernel, tq=tq, tk=tk, lam_init=lam_init), "diff_prompt", nb, seq, tq,
        [(r3(p['aq']), 'q'), (r3(p['ka']), 'seq'), (lw['a_lam'], 'const'), (lw['sg4'], 'const')],
        [tile(), tile()])
    nsel = min(DSA_TOPK, seq // 4)
    o_b = _prompt_call(
        functools.partial(_dsa_prompt_kernel, tq=tq, tk=tk, nsel=nsel, nbits=max(1, (seq - 1).bit_length())),
        "dsa_prompt", nb, seq, tq,
        [(r3(p['bq']), 'q'), (r3(p['iq']), 'q'), (r3(p['iw']), 'q'), (r3(p['kb']), 'seq')],
        [tile(), head_tiles(), tile(), pltpu.VMEM((seq // tk, tq, tk), I32)])
    assert seq % MOBA_BLOCK == 0 and MOBA_BLOCK % tq == 0 and seq // MOBA_BLOCK <= LANE
    o_c = _prompt_call(
        functools.partial(_moba_prompt_kernel, tq=tq, nsel=min(MOBA_TOPK, seq // MOBA_BLOCK)),
        "moba_prompt", nb, seq, tq,
        [(r3(p['cq']), 'q'), (r3(p['kc']), 'seq'), (r3(p['rc']), 'seq')],
        [tile(), head_tiles(), pltpu.VMEM((LANE, GROUP_WIDTH), BF16), pltpu.VMEM((LANE, GROUP_WIDTH), BF16)])
    o_d = _prompt_call(
        functools.partial(_mla_prompt_kernel, tq=tq, tk=tk), "mla_prompt", nb, seq, tq,
        [(r3(p['qd']), 'q'), (r3(p['kd']), 'seq'), (lw['wuv'], 'const')], [])
    return [o.reshape(nb * seq, GROUP_WIDTH) for o in (o_a, o_b, o_c, o_d)]


TOK_PAD = 8
NEW_PAD = 128


def _feature_major(buf):
    return len(buf.shape) == 4


def _fetch_pages(pt_ref, cache_ref, buf, sem, *, layer, npages, page):
    b = pl.program_id(0)
    slot = b % 2

    def copy(bb, sl, j):
        if _feature_major(buf):
            dst = buf.at[sl, j]
        else:
            dst = buf.at[sl, pl.ds(pl.multiple_of(j * page, page), page), :]
        return pltpu.make_async_copy(cache_ref.at[layer, pt_ref[bb * npages + j]], dst, sem.at[sl])

    def issue(bb, sl):
        def body(j, carry):
            copy(bb, sl, j).start()
            return carry
        lax.fori_loop(0, npages, body, 0)

    @pl.when(b == 0)
    def _():
        issue(0, 0)

    @pl.when(b + 1 < pl.num_programs(0))
    def _():
        issue(b + 1, 1 - slot)

    pltpu.make_async_copy(buf.at[slot], buf.at[slot], sem.at[slot]).wait()
    return slot


def _new_keys(new_ref, buf):
    new = new_ref[0]
    if _feature_major(buf):
        return new.astype(BF16)
    return jnp.concatenate([new, jnp.zeros((NEW_PAD - TOK_PAD, new.shape[1]), F32)], axis=0).astype(BF16)


def _key_chunk(buf, slot, c, ck):
    if _feature_major(buf):
        per = ck // buf.shape[3]
        return jnp.concatenate([buf[slot, c * per + i] for i in range(per)], axis=1).astype(BF16)
    return buf[slot, c * ck:(c + 1) * ck, :].astype(BF16)


def _qk(qx, kc, buf):
    return _dot(qx, kc) if _feature_major(buf) else _dot_nt(qx, kc)


def _pv(p, kc, buf):
    return _dot_nt(p, kc) if _feature_major(buf) else _dot(p, kc)


def _new_valid(rows, ntok):
    tok = lax.broadcasted_iota(I32, (rows, 1), 0) % TOK_PAD
    i = lax.broadcasted_iota(I32, (1, NEW_PAD), 1)
    return (i <= tok) & (i < ntok)


def _scores_to(s_ref, qx, buf, slot, new, npast, ck, scale):
    for c in range(npast // ck):
        s_ref[:, c * ck:(c + 1) * ck] = _qk(qx, _key_chunk(buf, slot, c, ck), buf) * scale
    s_ref[:, npast:npast + NEW_PAD] = _qk(qx, new, buf) * scale


def _softmax_rows(s_ref, ck):
    n = s_ref.shape[1]
    bounds = [(a, min(a + ck, n)) for a in range(0, n, ck)]
    m = jnp.full((s_ref.shape[0], 1), NEG, F32)
    for a, b in bounds:
        m = jnp.maximum(m, jnp.max(s_ref[:, a:b], axis=-1, keepdims=True))
    l = jnp.zeros((s_ref.shape[0], 1), F32)
    for a, b in bounds:
        p = jnp.exp(s_ref[:, a:b] - m)
        s_ref[:, a:b] = p
        l = l + jnp.sum(p, axis=-1, keepdims=True)
    return 1.0 / l


def _weighted_values(p_fn, buf, slot, new, npast, ck):
    acc = _pv(p_fn(npast, npast + NEW_PAD).astype(BF16), new, buf)
    for c in range(npast // ck):
        acc = acc + _pv(p_fn(c * ck, (c + 1) * ck).astype(BF16), _key_chunk(buf, slot, c, ck), buf)
    return acc


def _diff_sample_kernel(pt_ref, q_ref, new_ref, al_ref, sg_ref, cache_ref, o_ref, buf, sem, s_ref,
                        *, layer, npages, page, ck, ntok, lam_init):
    slot = _fetch_pages(pt_ref, cache_ref, buf, sem, layer=layer, npages=npages, page=page)
    npast = npages * page
    rows = 2 * N_HEADS * TOK_PAD
    half = rows // 2
    new = _new_keys(new_ref, buf)
    _scores_to(s_ref, q_ref[0], buf, slot, new, npast, ck, A_HEAD_DIM ** -0.5)
    s_ref[:, npast:] = jnp.where(_new_valid(rows, ntok), s_ref[:, npast:], NEG)
    rl = _softmax_rows(s_ref, ck)
    lam = _lambda_value(al_ref, lam_init)
    r1, r2 = rl[:half], rl[half:] * lam
    o = _weighted_values(lambda a, b: s_ref[0:half, a:b] * r1 - s_ref[half:rows, a:b] * r2, buf, slot, new, npast, ck)
    o_ref[0] = _subln(o, sg_ref[...], lam_init, o.shape[1], HEAD_W)


def _dsa_sample_kernel(pt_ref, q_ref, iq_ref, iw_ref, new_ref, cache_ref, o_ref, buf, sem, s_ref, k_ref,
                       *, layer, npages, page, ck, ntok, nsel, nbits):
    slot = _fetch_pages(pt_ref, cache_ref, buf, sem, layer=layer, npages=npages, page=page)
    npast = npages * page
    n = npast + NEW_PAD
    rows = N_HEADS * TOK_PAD
    new = _new_keys(new_ref, buf)
    iw = iw_ref[0]
    wcols = [iw[:, h:h + 1] * (IDX_DIM ** -0.5 * N_HEADS ** -0.5) for h in range(N_HEADS)]
    iqx = iq_ref[0]

    def index_scores(kc):
        si = _qk(iqx, kc, buf)
        sc = jnp.zeros((TOK_PAD, si.shape[1]), F32)
        for h in range(N_HEADS):
            sc = sc + jnp.maximum(si[h * TOK_PAD:(h + 1) * TOK_PAD], 0.0) * wcols[h]
        return _sort_key(sc)

    for c in range(npast // ck):
        k_ref[:, c * ck:(c + 1) * ck] = index_scores(_key_chunk(buf, slot, c, ck))
    new_ok = _new_valid(TOK_PAD, ntok)
    k_ref[:, npast:] = jnp.where(new_ok, index_scores(new), INT_MIN)

    def count(preds):
        nacc = 4
        accs = [[jnp.zeros((TOK_PAD, LANE), F32) for _ in range(nacc)] for _ in preds]
        for j in range(n // LANE):
            kc = k_ref[:, j * LANE:(j + 1) * LANE]
            for a, pred in zip(accs, preds):
                a[j % nacc] = a[j % nacc] + jnp.where(pred(kc), 1.0, 0.0)
        return [jnp.sum((a[0] + a[1]) + (a[2] + a[3]), axis=-1, keepdims=True) for a in accs]

    lane = lax.broadcasted_iota(I32, (1, ck), 1)

    def rewrite(t):
        for c in range(npast // ck):
            kc = k_ref[:, c * ck:(c + 1) * ck]
            k_ref[:, c * ck:(c + 1) * ck] = jnp.where(kc > t, -1, jnp.where(kc == t, c * ck + lane, BIG))
        kc = k_ref[:, npast:]
        code = jnp.where(kc > t, -1, jnp.where(kc == t, npast + lane[:, :NEW_PAD], BIG))
        k_ref[:, npast:] = jnp.where(new_ok, code, BIG)

    bound = _select_topk(lambda cs: count([lambda kc, c=c: kc >= c for c in cs]),
                         lambda cs: count([lambda kc, c=c: kc < c for c in cs]),
                         rewrite, TOK_PAD, nsel, nbits, 2)

    _scores_to(s_ref, q_ref[0], buf, slot, new, npast, ck, HEAD_W ** -0.5)
    bounds = [(a, min(a + ck, n)) for a in range(0, n, ck)]
    for a, b in bounds:
        picked = k_ref[:, a:b] <= bound
        for h in range(N_HEADS):
            r = slice(h * TOK_PAD, (h + 1) * TOK_PAD)
            s_ref[r, a:b] = jnp.where(picked, s_ref[r, a:b], NEG)
    rl = _softmax_rows(s_ref, ck)
    o_ref[0] = _weighted_values(lambda a, b: s_ref[:, a:b] * rl, buf, slot, new, npast, ck)


def _moba_sample_kernel(pt_ref, q_ref, new_ref, cache_ref, o_ref, buf, sem, s_ref,
                        *, layer, npages, page, ck, ntok, nsel):
    slot = _fetch_pages(pt_ref, cache_ref, buf, sem, layer=layer, npages=npages, page=page)
    npast = npages * page
    rows = N_HEADS * TOK_PAD
    own = npast // MOBA_BLOCK
    new = _new_keys(new_ref, buf)
    qx = q_ref[0]
    km = jnp.zeros((LANE, buf.shape[2]), F32)
    for c in range(npast // ck):
        km = km + _block_means(buf[slot, c * ck:(c + 1) * ck, :], c * ck)
    hi, lo = _split_bf16(km)
    gate = _dot_nt(qx, hi) + _dot_nt(qx, lo)
    sel = jnp.where(_top_blocks(gate, own, nsel), 1.0, 0.0).astype(BF16)
    _scores_to(s_ref, qx, buf, slot, new, npast, ck, HEAD_W ** -0.5)
    for c in range(npast // ck):
        j = lax.broadcasted_iota(I32, (LANE, ck), 0)
        s = lax.broadcasted_iota(I32, (LANE, ck), 1)
        expand = jnp.where((c * ck + s) // MOBA_BLOCK == j, 1.0, 0.0).astype(BF16)
        picked = _dot(sel, expand) > 0.5
        s_ref[:, c * ck:(c + 1) * ck] = jnp.where(picked, s_ref[:, c * ck:(c + 1) * ck], NEG)
    s_ref[:, npast:] = jnp.where(_new_valid(rows, ntok), s_ref[:, npast:], NEG)
    rl = _softmax_rows(s_ref, ck)
    o_ref[0] = _weighted_values(lambda a, b: s_ref[:, a:b] * rl, buf, slot, new, npast, ck)


def _mla_sample_kernel(pt_ref, q_ref, new_ref, wuv_ref, cache_ref, o_ref, buf, sem, s_ref,
                       *, layer, npages, page, ck, ntok):
    slot = _fetch_pages(pt_ref, cache_ref, buf, sem, layer=layer, npages=npages, page=page)
    npast = npages * page
    rows = N_HEADS * TOK_PAD
    new = _new_keys(new_ref, buf)
    _scores_to(s_ref, q_ref[0], buf, slot, new, npast, ck, (D_NOPE + D_ROPE) ** -0.5)
    s_ref[:, npast:] = jnp.where(_new_valid(rows, ntok), s_ref[:, npast:], NEG)
    rl = _softmax_rows(s_ref, ck)
    ol = _weighted_values(lambda a, b: s_ref[:, a:b] * rl, buf, slot, new, npast, ck)
    o = jnp.zeros((TOK_PAD, GROUP_WIDTH), F32)
    for h in range(N_HEADS):
        o = o + _dot(ol[h * TOK_PAD:(h + 1) * TOK_PAD].astype(BF16), wuv_ref[h])
    o_ref[0] = o


def _sample_call(kernel, name, page_table, inputs, cache, out_rows, out_w, scratch):
    nb = page_table.shape[0]
    in_specs = []
    for arr, kind in inputs:
        if kind == 'b':
            in_specs.append(pl.BlockSpec((1,) + arr.shape[1:], lambda b, pt: (b, 0, 0)))
        else:
            nd = arr.ndim
            in_specs.append(pl.BlockSpec(arr.shape, lambda b, pt, nd=nd: (0,) * nd))
    in_specs.append(pl.BlockSpec(memory_space=pl.ANY))
    return pl.pallas_call(
        kernel,
        grid_spec=pltpu.PrefetchScalarGridSpec(
            num_scalar_prefetch=1,
            grid=(nb,),
            in_specs=in_specs,
            out_specs=pl.BlockSpec((1, out_rows, out_w), lambda b, pt: (b, 0, 0)),
            scratch_shapes=scratch,
        ),
        out_shape=jax.ShapeDtypeStruct((nb, out_rows, out_w), F32),
        compiler_params=pltpu.CompilerParams(dimension_semantics=("arbitrary",), vmem_limit_bytes=VMEM_LIMIT),
        name=name,
    )(page_table.reshape(-1), *[a for a, _ in inputs], cache)


def _head_rows(q, nb, ntok, lane_lo, width):
    w = q.shape[-1] // N_HEADS
    q = q.reshape(nb, ntok, N_HEADS, w).transpose(0, 2, 1, 3)
    q = jnp.pad(q, ((0, 0), (0, 0), (0, TOK_PAD - ntok), (lane_lo, width - lane_lo - w)))
    return q.reshape(nb, N_HEADS * TOK_PAD, width)


def _unhead_rows(o, nb, ntok, lane_lo):
    o = o.reshape(nb, N_HEADS, TOK_PAD, o.shape[-1])[:, :, :ntok, lane_lo:lane_lo + HEAD_W]
    return o.transpose(0, 2, 1, 3).reshape(nb * ntok, GROUP_WIDTH).astype(BF16)


def _sample_attention(s, lw, caches, page_table, layer, ntok, lam_init):
    nb, npages = page_table.shape
    page = caches[0].shape[2]
    npast = npages * page
    ck = min(1024, npast)
    n = npast + NEW_PAD
    assert npast % ck == 0 and npast % MOBA_BLOCK == 0 and ntok <= TOK_PAD and npast // MOBA_BLOCK < LANE
    new_rows = lambda r: jnp.pad(r.reshape(nb, ntok, r.shape[-1]), ((0, 0), (0, TOK_PAD - ntok), (0, 0)))
    new_cols = lambda r: jnp.pad(r.reshape(nb, ntok, r.shape[-1]), ((0, 0), (0, NEW_PAD - ntok), (0, 0))).transpose(0, 2, 1)
    common = dict(layer=layer, npages=npages, page=page, ck=ck, ntok=ntok)
    sem = pltpu.SemaphoreType.DMA((2,))
    rows = N_HEADS * TOK_PAD

    wa = CACHE_W[0]
    aq = s['aq'].reshape(nb, ntok, N_HEADS, 2, A_HEAD_DIM).transpose(0, 3, 2, 1, 4)
    aq = jnp.pad(aq, ((0, 0), (0, 0), (0, 0), (0, TOK_PAD - ntok), (0, 0)))
    aq = jnp.stack([jnp.pad(aq[:, m], ((0, 0), (0, 0), (0, 0), (m * A_HEAD_DIM, wa - (m + 1) * A_HEAD_DIM)))
                    for m in range(2)], axis=1).reshape(nb, 2 * rows, wa)
    o_a = _sample_call(
        functools.partial(_diff_sample_kernel, lam_init=lam_init, **common), "diff_sample", page_table,
        [(aq, 'b'), (new_rows(s['ra']), 'b'), (lw['a_lam'], 'const'), (lw['sg_pad'], 'const')], caches[0], rows, wa,
        [pltpu.VMEM((2, npast, wa), F32), sem, pltpu.VMEM((2 * rows, n), F32)])

    wb = CACHE_W[1]
    iw = jnp.pad(s['iw'][:, HEAD_W:HEAD_W + N_HEADS].reshape(nb, ntok, N_HEADS),
                 ((0, 0), (0, TOK_PAD - ntok), (0, LANE - N_HEADS)))
    nsel = min(DSA_TOPK, (npast + ntok) // 4)
    o_b = _sample_call(
        functools.partial(_dsa_sample_kernel, nsel=nsel, nbits=max(1, (n - 1).bit_length()), **common),
        "dsa_sample", page_table,
        [(_head_rows(s['bq'], nb, ntok, 0, wb), 'b'), (_head_rows(s['iq'], nb, ntok, 2 * HEAD_W, wb), 'b'),
         (iw, 'b'), (new_cols(s['rb']), 'b')], caches[1], rows, wb,
        [pltpu.VMEM((2, npages, wb, page), F32), sem, pltpu.VMEM((rows, n), F32), pltpu.VMEM((TOK_PAD, n), I32)])

    wc = CACHE_W[2]
    o_c = _sample_call(
        functools.partial(_moba_sample_kernel, nsel=min(MOBA_TOPK, npast // MOBA_BLOCK + 1), **common),
        "moba_sample", page_table,
        [(_head_rows(s['cq'], nb, ntok, 0, wc), 'b'), (new_rows(s['rc']), 'b')], caches[2], rows, wc,
        [pltpu.VMEM((2, npast, wc), F32), sem, pltpu.VMEM((rows, n), F32)])

    wd = CACHE_W[3]
    qd = s['qd'].reshape(nb * ntok, N_HEADS, QX_W)[:, :, :wd].reshape(nb * ntok, N_HEADS * wd)
    o_d = _sample_call(
        functools.partial(_mla_sample_kernel, **common), "mla_sample", page_table,
        [(_head_rows(qd, nb, ntok, 0, wd), 'b'), (new_cols(s['rd']), 'b'), (lw['wuv_s'], 'const')], caches[3],
        TOK_PAD, GROUP_WIDTH,
        [pltpu.VMEM((2, npages, wd, page), F32), sem, pltpu.VMEM((rows, n), F32)])

    return [_unhead_rows(o_a, nb, ntok, HEAD_W), _unhead_rows(o_b, nb, ntok, HEAD_W),
            _unhead_rows(o_c, nb, ntok, HEAD_W),
            o_d[:, :ntok].reshape(nb * ntok, GROUP_WIDTH).astype(BF16)]


def _prep_weights(attn_norm_g, w_in, a_lambda, a_subln_g, d_q_norm_g, d_kv_norm_g, d_w_uq, d_w_uk, d_w_uv,
                  w_out, ffn_norm_g, w_gate, w_up, w_down):
    depth = w_in.shape[0]
    src, swp, _, _ = _build_layout()
    w_pad = jnp.pad(w_in, ((0, 0), (0, 0), (0, 1)))
    w_cat = jnp.take(w_pad, jnp.asarray(np.concatenate([src, swp])), axis=2).astype(BF16)
    qsrc, qswp, _, _ = _build_q_layout()
    uq_pad = jnp.pad(d_w_uq, ((0, 0), (0, 0), (0, 1)))
    wuq = jnp.take(uq_pad, jnp.asarray(np.concatenate([qsrc, qswp])), axis=2).astype(BF16)
    eye_h = jnp.eye(N_HEADS, dtype=F32)
    nope = jnp.einsum('lchn,hg->lhngc', d_w_uk, eye_h)
    nope = jnp.pad(nope, ((0, 0),) * 4 + ((0, QX_W - KV_LORA),)).reshape(depth, N_HEADS * D_NOPE, N_HEADS * QX_W)
    rope = np.zeros((N_HEADS, D_ROPE, N_HEADS, QX_W), np.float32)
    for h in range(N_HEADS):
        rope[h, np.arange(D_ROPE), h, KV_LORA + np.arange(D_ROPE)] = 1.0
    rope = jnp.broadcast_to(jnp.asarray(rope.reshape(N_HEADS * D_ROPE, N_HEADS * QX_W))[None],
                            (depth, N_HEADS * D_ROPE, N_HEADS * QX_W))
    bd = jnp.concatenate([nope, rope], axis=1).astype(BF16)
    wuv = jnp.einsum('lchv,hg->lhcgv', d_w_uv, eye_h).reshape(depth, N_HEADS, KV_LORA, GROUP_WIDTH)
    wuv_s = jnp.pad(wuv, ((0, 0), (0, 0), (0, CACHE_W[3] - KV_LORA), (0, 0))).astype(BF16)
    wuv = jnp.pad(wuv, ((0, 0), (0, 0), (0, QX_W - KV_LORA), (0, 0))).astype(BF16)
    layers = []
    for l in range(depth):
        layers.append(dict(
            attn_g=attn_norm_g[l][None], w_cat=w_cat[l], gq=d_q_norm_g[l][None], gkv=d_kv_norm_g[l][None],
            wuq=wuq[l], bd=bd[l], wuv=wuv[l], wuv_s=wuv_s[l], a_lam=a_lambda[l],
            sg4=jnp.tile(a_subln_g[l], N_HEADS)[None],
            sg_pad=jnp.pad(a_subln_g[l], (HEAD_W, 0))[None],
            wo=w_out[l].astype(BF16), ffn_g=ffn_norm_g[l][None],
            wg=w_gate[l].astype(BF16), wu=w_up[l].astype(BF16), wd=w_down[l].astype(BF16)))
    return layers


def kernel(x_prompt, x_sample, cache_diff_kv, cache_dsa_kv, cache_moba_kv, cache_mla_latent, page_table,
           attn_norm_g, w_in, a_lambda, a_subln_g, d_q_norm_g, d_kv_norm_g, d_w_uq, d_w_uk, d_w_uv,
           w_out, ffn_norm_g, w_gate, w_up, w_down, final_norm_g):
    nb, seq, d = x_prompt.shape
    db, ntok, _ = x_sample.shape
    depth = w_in.shape[0]
    caches = (cache_diff_kv, jnp.swapaxes(cache_dsa_kv, 2, 3), cache_moba_kv, jnp.swapaxes(cache_mla_latent, 2, 3))
    past_len = page_table.shape[1] * cache_diff_kv.shape[2]
    layers = _prep_weights(attn_norm_g, w_in, a_lambda, a_subln_g, d_q_norm_g, d_kv_norm_g, d_w_uq, d_w_uk,
                           d_w_uv, w_out, ffn_norm_g, w_gate, w_up, w_down)
    _, _, fidx, sgn = _build_layout()
    _, _, qfidx, qsgn = _build_q_layout()
    pos_p = jnp.arange(seq, dtype=jnp.int32)
    pos_s = jnp.tile(past_len + jnp.arange(ntok, dtype=jnp.int32), db)
    tabs_p = _rope_tables(pos_p, fidx, sgn) + _rope_tables(pos_p, qfidx, qsgn)
    tabs_s = _rope_tables(pos_s, fidx, sgn) + _rope_tables(pos_s, qfidx, qsgn)
    tabs_p = (tabs_p[0], tabs_p[1], tabs_p[2], tabs_p[3])
    tabs_s = (tabs_s[0], tabs_s[1], tabs_s[2], tabs_s[3])
    ntab_p = seq // _token_tile(nb * seq) if seq % _token_tile(nb * seq) == 0 else None
    assert ntab_p is not None
    ntab_s = (db * ntok) // _token_tile(db * ntok)

    hp = x_prompt.reshape(nb * seq, d)
    hs = x_sample.reshape(db * ntok, d)
    rows_p = {k: [] for k in ('ra', 'rb', 'rc', 'rd')}
    rows_s = {k: [] for k in ('ra', 'rb', 'rc', 'rd')}
    y_p = y_s = None
    for l in range(depth):
        lam_init = 0.8 - 0.6 * math.exp(-0.3 * l)
        lw = layers[l]
        fin = final_norm_g[None] if l == depth - 1 else None
        p = _inproj(hp, tabs_p, lw, ntab_p)
        o_p = _prompt_attention(p, lw, nb, seq, lam_init)
        hp, y_p = _ffn(hp, o_p, lw, fin)
        s = _inproj(hs, tabs_s, lw, ntab_s)
        o_s = _sample_attention(s, lw, caches, page_table, l, ntok, lam_init)
        hs, y_s = _ffn(hs, o_s, lw, fin)
        for k in rows_p:
            rows_p[k].append(p[k].reshape(nb, seq, -1))
            rows_s[k].append(s[k].reshape(db, ntok, -1))
    stack = lambda xs: jnp.stack(xs)
    return (y_p.reshape(nb, seq, d), y_s.reshape(db, ntok, d),
            stack(rows_p['ra']), stack(rows_p['rb']), stack(rows_p['rc']), stack(rows_p['rd']),
            stack(rows_s['ra']), stack(rows_s['rb']), stack(rows_s['rc']), stack(rows_s['rd']))
```

```python
import functools
import math

import numpy as np
import jax
import jax.numpy as jnp
from jax import lax
from jax.experimental import pallas as pl
from jax.experimental.pallas import tpu as pltpu

F32, BF16, I32 = jnp.float32, jnp.bfloat16, jnp.int32

N_HEADS = 4
GROUP_WIDTH = 256
HEAD_W = 64
A_HEAD_DIM = 32
IDX_DIM = 64
DSA_TOPK = 256
MOBA_BLOCK = 256
MOBA_TOPK = 3
D_NOPE, D_ROPE, KV_LORA, Q_LORA = 64, 32, 128, 256
ROPE_THETA = 500000.0
EPS = 1e-6
CACHE_W = (128, 192, 128, 160)

_NAMES = ('aq', 'ak', 'av', 'bq', 'bk', 'bv', 'iq', 'ik', 'iw', 'cq', 'ck', 'cv', 'dcq', 'dckv', 'dkr')
_SIZES = (256, 64, 64, 256, 64, 64, 256, 64, 4, 256, 64, 64, 256, 128, 32)
_ORIG = dict(zip(_NAMES, (int(v) for v in np.cumsum((0,) + _SIZES[:-1]))))
IN_WIDTH = sum(_SIZES)
_NEW = {'aq': 0, 'bq': 256, 'iq': 512, 'cq': 768, 'dcq': 1024, 'ak': 1280, 'av': 1344, 'ck': 1408, 'cv': 1472,
        'bk': 1536, 'bv': 1600, 'ik': 1664, 'iw': 1728, 'dckv': 1792, 'dkr': 1920}
WP = 2048
_ROT = {'aq': (32, 8), 'ak': (32, 8), 'bq': (64, 16), 'bk': (64, 16), 'iq': (64, 16), 'ik': (64, 16),
        'cq': (64, 16), 'ck': (64, 16), 'dkr': (32, 32)}
_ROT_DIMS = (8, 16, 32)
_FBASE = {8: 0, 16: 4, 32: 12}
QD_W = N_HEADS * (D_NOPE + D_ROPE)
QX_W = 256

LANE = 128
INT_MIN = -2 ** 31
NEG = -1e30
BIG = 2 ** 30
VMEM_LIMIT = 56 * 1024 * 1024


def _build_layout():
    src = np.full(WP, IN_WIDTH, np.int32)
    swp = np.full(WP, IN_WIDTH, np.int32)
    fidx = np.zeros(WP, np.int32)
    sgn = np.zeros(WP, np.float32)
    for name, size in zip(_NAMES, _SIZES):
        for j in range(size):
            c, o = _NEW[name] + j, _ORIG[name] + j
            src[c] = o
            if name in _ROT:
                gw, r = _ROT[name]
                d, half = j % gw, r // 2
                if d < r:
                    first = d < half
                    swp[c] = o + half if first else o - half
                    fidx[c] = _FBASE[r] + d % half
                    sgn[c] = -1.0 if first else 1.0
    return src, swp, fidx, sgn


def _build_q_layout():
    per = D_NOPE + D_ROPE
    src = np.zeros(QD_W, np.int32)
    swp = np.full(QD_W, QD_W, np.int32)
    fidx = np.zeros(QD_W, np.int32)
    sgn = np.zeros(QD_W, np.float32)
    half = D_ROPE // 2
    for h in range(N_HEADS):
        for n in range(D_NOPE):
            src[h * D_NOPE + n] = h * per + n
        for r in range(D_ROPE):
            c, o = N_HEADS * D_NOPE + h * D_ROPE + r, h * per + D_NOPE + r
            src[c] = o
            first = r < half
            swp[c] = o + half if first else o - half
            fidx[c] = _FBASE[D_ROPE] + r % half
            sgn[c] = -1.0 if first else 1.0
    return src, swp, fidx, sgn


def _rope_tables(pos, fidx, sgn):
    inv = jnp.concatenate([ROPE_THETA ** (-(jnp.arange(r // 2, dtype=F32) * 2.0 / r)) for r in _ROT_DIMS])
    ang = pos.astype(F32)[:, None] * inv[None, :]
    cos_f, sin_f = jnp.cos(ang), jnp.sin(ang)
    rot = jnp.asarray(sgn != 0)
    cos = jnp.where(rot[None, :], cos_f[:, fidx], 1.0)
    sin = sin_f[:, fidx] * jnp.asarray(sgn)[None, :]
    return cos, sin


def _dot(a, b):
    return jnp.dot(a, b, preferred_element_type=F32)


def _dot_nt(a, b):
    return lax.dot_general(a, b, (((1,), (1,)), ((), ())), preferred_element_type=F32)


def _group_mask(width, group, idx):
    lane = lax.broadcasted_iota(I32, (1, width), 1)
    return (lane // group) == idx


def _tile4(x, base):
    w = x.shape[1]
    j = lax.broadcasted_iota(I32, (w, GROUP_WIDTH), 0)
    c = lax.broadcasted_iota(I32, (w, GROUP_WIDTH), 1)
    rep = jnp.where(j == base + (c % HEAD_W), 1.0, 0.0).astype(BF16)
    return _dot(x, rep).astype(BF16)


def _split_bf16(x):
    hi = x.astype(BF16)
    lo = (x - hi.astype(F32)).astype(BF16)
    return hi, lo


def _rms(x, g):
    return x * lax.rsqrt(jnp.mean(x * x, axis=-1, keepdims=True) + EPS) * g


def _sort_key(x):
    b = lax.bitcast_convert_type(x, I32)
    return b ^ ((b >> 31) & 0x7FFFFFFF)


def _by_lane(cols, lane_masks):
    out = cols[-1]
    for col, mask in zip(cols[-2::-1], lane_masks[-2::-1]):
        out = jnp.where(mask, col, out)
    return out


def _flash(qhs, k_ref, v_loads, wv, nkv, tk, scale, mask_fns, groups=None, lane_masks=None):
    tq = qhs[0].shape[0]
    n = len(qhs)
    if not isinstance(mask_fns, (list, tuple)):
        mask_fns = [mask_fns] * n
    if groups is None:
        groups = [[i] for i in range(n)]

    def spread(cols, g):
        if len(g) == 1:
            return cols[g[0]]
        return _by_lane([cols[i] for i in g], [lane_masks[i] for i in g])

    def body(c, carry):
        ms, ls, accs = carry
        off = pl.multiple_of(c * tk, tk)
        k = k_ref[pl.ds(off, tk), :]
        masks = {}
        new_m, new_l, alphas, pvs = [], [], [], []
        for i in range(n):
            fn = mask_fns[i]
            if fn not in masks:
                masks[fn] = fn(c)
            s = _dot_nt(qhs[i], k)
            s = jnp.where(masks[fn], s if scale is None else s * scale, NEG)
            m_new = jnp.maximum(ms[i], jnp.max(s, axis=-1, keepdims=True))
            alpha = jnp.exp(ms[i] - m_new)
            p = jnp.exp(s - m_new)
            new_m.append(m_new)
            new_l.append(alpha * ls[i] + jnp.sum(p, axis=-1, keepdims=True))
            alphas.append(alpha)
            pvs.append(_dot(p.astype(BF16), v_loads[i](off)))
        new_acc = []
        for g, acc in zip(groups, accs):
            pv = pvs[g[0]]
            for i in g[1:]:
                pv = pv + pvs[i]
            new_acc.append(spread(alphas, g) * acc + pv)
        return tuple(new_m), tuple(new_l), tuple(new_acc)

    init = (tuple(jnp.full((tq, 1), NEG, F32) for _ in range(n)), tuple(jnp.zeros((tq, 1), F32) for _ in range(n)),
            tuple(jnp.zeros((tq, wv), F32) for _ in groups))
    _, ls, accs = lax.fori_loop(0, nkv, body, init)
    return [acc / spread(ls, g) for g, acc in zip(groups, accs)]


def _causal_fn(qi, tq, tk):
    qpos = qi * tq + lax.broadcasted_iota(I32, (tq, 1), 0)
    lane = lax.broadcasted_iota(I32, (1, tk), 1)
    return lambda c: (c * tk + lane) <= qpos


def _top_blocks(g, own, nsel):
    lane = lax.broadcasted_iota(I32, g.shape, 1)
    lane_f = lane.astype(F32)
    past = lane < own
    g = jnp.where(past, g, -jnp.inf)
    sel = jnp.zeros(g.shape, jnp.bool_)
    for _ in range(nsel):
        gm = jnp.where(sel, -jnp.inf, g)
        top = jnp.max(gm, axis=-1, keepdims=True)
        first = jnp.min(jnp.where(gm == top, lane_f, float(LANE)), axis=-1, keepdims=True)
        sel = sel | (lane_f == first)
    return sel & past


def _inproj_kernel(x_ref, g_ref, w_ref, cos_ref, sin_ref, gq_ref, gkv_ref, wuq_ref, cosq_ref, sinq_ref, bd_ref,
                   aq_o, bq_o, iq_o, cq_o, qd_o, iw_o, ra_o, rb_o, rc_o, rd_o, ka_o, kb_o, kc_o, kd_o):
    h = _rms(x_ref[...], g_ref[...]).astype(BF16)
    y2 = _dot(h, w_ref[...])
    y = y2[:, :WP] * cos_ref[...] + y2[:, WP:] * sin_ref[...]
    aq_o[...] = y[:, 0:256].astype(BF16)
    bq_o[...] = y[:, 256:512].astype(BF16)
    iq_o[...] = y[:, 512:768].astype(BF16)
    cq_o[...] = y[:, 768:1024].astype(BF16)
    dn = _rms(y[:, 1024:1280], gq_ref[...]).astype(BF16)
    q2 = _dot(dn, wuq_ref[...])
    qd = q2[:, :QD_W] * cosq_ref[...] + q2[:, QD_W:] * sinq_ref[...]
    qd_o[...] = _dot(qd.astype(BF16), bd_ref[...]).astype(BF16)
    ra = y[:, 1280:1408]
    ra_o[...] = ra
    ka_o[...] = ra.astype(BF16)
    rc = y[:, 1408:1536]
    rc_o[...] = rc
    kc_o[...] = rc.astype(BF16)
    rb = y[:, 1536:1792]
    rb_o[...] = rb[:, :192]
    kb_o[...] = rb.astype(BF16)
    iw_o[...] = rb[:, 128:256]
    rd = y[:, 1792:2048]
    ckv = _rms(rd[:, :KV_LORA], gkv_ref[...])
    rd_o[:, 0:KV_LORA] = ckv
    rd_o[:, KV_LORA:KV_LORA + D_ROPE] = rd[:, KV_LORA:KV_LORA + D_ROPE]
    kd_o[:, 0:KV_LORA] = ckv.astype(BF16)
    kd_o[:, KV_LORA:QX_W] = rd[:, KV_LORA:QX_W].astype(BF16)


_K1_OUT = (('aq', 256, BF16), ('bq', 256, BF16), ('iq', 256, BF16), ('cq', 256, BF16), ('qd', N_HEADS * QX_W, BF16),
           ('iw', 128, F32), ('ra', 128, F32), ('rb', 192, F32), ('rc', 128, F32), ('rd', 160, F32),
           ('ka', 128, BF16), ('kb', 256, BF16), ('kc', 128, BF16), ('kd', 256, BF16))


def _token_tile(n):
    return 256 if n % 256 == 0 else n


def _inproj(x2d, tabs, lw, ntab):
    n, d = x2d.shape
    tm = _token_tile(n)
    cos, sin, cosq, sinq = tabs
    row = lambda i: (i, 0)
    tab = lambda i: (i % ntab, 0)
    const = lambda i: (0, 0)
    one = pl.Buffered(1)
    in_specs = [
        pl.BlockSpec((tm, d), row),
        pl.BlockSpec((1, d), const),
        pl.BlockSpec((d, 2 * WP), const, pipeline_mode=one),
        pl.BlockSpec((tm, WP), tab),
        pl.BlockSpec((tm, WP), tab),
        pl.BlockSpec((1, Q_LORA), const),
        pl.BlockSpec((1, KV_LORA), const),
        pl.BlockSpec((Q_LORA, 2 * QD_W), const, pipeline_mode=one),
        pl.BlockSpec((tm, QD_W), tab),
        pl.BlockSpec((tm, QD_W), tab),
        pl.BlockSpec((QD_W, N_HEADS * QX_W), const, pipeline_mode=one),
    ]
    outs = pl.pallas_call(
        _inproj_kernel,
        grid=(n // tm,),
        in_specs=in_specs,
        out_specs=[pl.BlockSpec((tm, w), row) for _, w, _ in _K1_OUT],
        out_shape=[jax.ShapeDtypeStruct((n, w), dt) for _, w, dt in _K1_OUT],
        compiler_params=pltpu.CompilerParams(dimension_semantics=("arbitrary",), vmem_limit_bytes=VMEM_LIMIT),
        name="inproj",
    )(x2d, lw['attn_g'], lw['w_cat'], cos, sin, lw['gq'], lw['gkv'], lw['wuq'], cosq, sinq, lw['bd'])
    return {name: o for (name, _, _), o in zip(_K1_OUT, outs)}


def _ffn_kernel(x_ref, oa_ref, ob_ref, oc_ref, od_ref, wo_ref, g_ref, wg_ref, wu_ref, wd_ref, *rest, final):
    if final:
        gf_ref, x_out, y_out = rest
    else:
        (x_out,) = rest
    x = x_ref[...]
    for i, o_ref in enumerate((oa_ref, ob_ref, oc_ref, od_ref)):
        x = x + _dot(o_ref[...], wo_ref[i * GROUP_WIDTH:(i + 1) * GROUP_WIDTH, :])
    h = _rms(x, g_ref[...]).astype(BF16)
    gate = _dot(h, wg_ref[...])
    up = _dot(h, wu_ref[...])
    act = gate * (1.0 / (1.0 + jnp.exp(-gate))) * up
    x = x + _dot(act.astype(BF16), wd_ref[...])
    x_out[...] = x
    if final:
        y_out[...] = _rms(x, gf_ref[...])


def _ffn(x2d, o_parts, lw, final_g):
    n, d = x2d.shape
    tm = _token_tile(n)
    dff = lw['wg'].shape[1]
    final = final_g is not None
    row = lambda i: (i, 0)
    const = lambda i: (0, 0)
    one = pl.Buffered(1)
    in_specs = [pl.BlockSpec((tm, d), row)] + [pl.BlockSpec((tm, GROUP_WIDTH), row)] * 4 + [
        pl.BlockSpec((d, d), const, pipeline_mode=one),
        pl.BlockSpec((1, d), const),
        pl.BlockSpec((d, dff), const, pipeline_mode=one),
        pl.BlockSpec((d, dff), const, pipeline_mode=one),
        pl.BlockSpec((dff, d), const, pipeline_mode=one),
    ]
    args = [x2d, *o_parts, lw['wo'], lw['ffn_g'], lw['wg'], lw['wu'], lw['wd']]
    out_specs = [pl.BlockSpec((tm, d), row)]
    out_shape = [jax.ShapeDtypeStruct((n, d), F32)]
    if final:
        in_specs.append(pl.BlockSpec((1, d), const))
        args.append(final_g)
        out_specs.append(pl.BlockSpec((tm, d), row))
        out_shape.append(jax.ShapeDtypeStruct((n, d), F32))
    outs = pl.pallas_call(
        functools.partial(_ffn_kernel, final=final),
        grid=(n // tm,),
        in_specs=in_specs,
        out_specs=out_specs,
        out_shape=out_shape,
        compiler_params=pltpu.CompilerParams(dimension_semantics=("arbitrary",), vmem_limit_bytes=VMEM_LIMIT),
        name="outproj_ffn",
    )(*args)
    return outs if final else (outs[0], None)


def _num_chunks(qi, tq, tk):
    return ((qi + 1) * tq + tk - 1) // tk


def _lambda_value(al_ref, lam_init):
    al = al_ref[...]
    s1 = jnp.sum(al[0:1, :] * al[1:2, :], axis=-1, keepdims=True)
    s2 = jnp.sum(al[2:3, :] * al[3:4, :], axis=-1, keepdims=True)
    return jnp.exp(s1) - jnp.exp(s2) + lam_init


def _subln(o, g, lam_init, width, first_lane):
    msb = jnp.zeros(o.shape, F32)
    for hh in range(first_lane // HEAD_W, width // HEAD_W):
        mh = _group_mask(width, HEAD_W, hh)
        ms = jnp.sum(jnp.where(mh, o * o, 0.0), axis=-1, keepdims=True) * (1.0 / HEAD_W)
        msb = msb + jnp.where(mh, ms, 0.0)
    return o * lax.rsqrt(msb + EPS) * g * (1.0 - lam_init)


def _store_head_values(vt, kv, base):
    tile = _tile4(kv, base)
    for h in range(N_HEADS):
        vt[h] = jnp.where(_group_mask(GROUP_WIDTH, HEAD_W, h), tile, 0).astype(BF16)


def _head_value_loads(vt, tk):
    return [lambda off, h=h: vt[h, pl.ds(off, tk), :] for h in range(N_HEADS)]


def _diff_prompt_kernel(q_ref, kv_ref, al_ref, sg_ref, o_ref, kt, vt, *, tq, tk, lam_init):
    qi = pl.program_id(1)

    @pl.when(qi == 0)
    def _():
        kv = kv_ref[0]
        kt[...] = _tile4(kv, 0)
        vt[...] = _tile4(kv, HEAD_W)

    nkv = _num_chunks(qi, tq, tk)
    causal = _causal_fn(qi, tq, tk)
    lam = _lambda_value(al_ref, lam_init)
    q = q_ref[0]
    scale = A_HEAD_DIM ** -0.5
    values = lambda off: vt[pl.ds(off, tk), :]
    o = jnp.zeros((tq, GROUP_WIDTH), F32)
    for h0 in range(0, N_HEADS, 2):
        qg = [jnp.where(_group_mask(GROUP_WIDTH, A_HEAD_DIM, 2 * h0 + g), q, 0).astype(BF16) for g in range(4)]
        og = _flash(qg, kt, [values] * 4, GROUP_WIDTH, nkv, tk, scale, causal)
        for j in range(2):
            o = o + jnp.where(_group_mask(GROUP_WIDTH, HEAD_W, h0 + j), og[2 * j] - lam * og[2 * j + 1], 0.0)
    o_ref[0] = _subln(o, sg_ref[...], lam_init, GROUP_WIDTH, 0).astype(BF16)


def _count_cols(keys_ref, nkv, preds):
    tk, cols = keys_ref.shape[1:]
    nacc = 8

    def body(c, accs):
        kc = keys_ref[c]
        out = []
        for acc, p in zip(accs, preds):
            m = jnp.where(p(kc), 1.0, 0.0)
            acc = list(acc)
            for g in range(tk // 8):
                acc[g % nacc] = acc[g % nacc] + m[g * 8:(g + 1) * 8, :]
            out.append(tuple(acc))
        return tuple(out)

    init = tuple(tuple(jnp.zeros((8, cols), F32) for _ in range(nacc)) for _ in preds)
    res = []
    for acc in lax.fori_loop(0, nkv, body, init):
        tot = ((acc[0] + acc[1]) + (acc[2] + acc[3])) + ((acc[4] + acc[5]) + (acc[6] + acc[7]))
        res.append(jnp.sum(tot, axis=0, keepdims=True))
    return res


def _select_topk(count_ge, count_lt, rewrite, shape, k, nbits, bits_per_step):
    kf = float(k)
    ncand = (1 << bits_per_step) - 1

    def bit_body(i, t):
        shift = 32 - bits_per_step * (i + 1)
        cands = [t | lax.shift_left(jnp.int32(j), shift) for j in range(1, ncand + 1)]
        counts = count_ge([c ^ INT_MIN for c in cands])
        for c, n in zip(cands, counts):
            t = jnp.where(n >= kf, c, t)
        return t

    t = lax.fori_loop(0, 32 // bits_per_step, bit_body, jnp.zeros(shape, I32)) ^ INT_MIN
    rewrite(t)

    def pos_body(i, b):
        cand = b | lax.shift_left(jnp.int32(1), nbits - 1 - i)
        return jnp.where(count_lt([cand])[0] < kf, cand, b)

    def search():
        return lax.fori_loop(0, nbits, pos_body, jnp.zeros(shape, I32))

    tied = jnp.max(count_lt([jnp.full(shape, BIG, I32)])[0]) > kf
    return lax.cond(tied, search, lambda: jnp.full(shape, BIG - 1, I32))


def _dsa_prompt_kernel(q_ref, iq_ref, iwt_ref, kv_ref, o_ref, kt, vt, it, keys, sel, *, tq, tk, nsel, nbits):
    qi = pl.program_id(1)

    @pl.when(qi == 0)
    def _():
        kv = kv_ref[0]
        kt[...] = _tile4(kv, 0)
        _store_head_values(vt, kv, HEAD_W)
        it[...] = _tile4(kv, 2 * HEAD_W)

    nkv = _num_chunks(qi, tq, tk)
    iq = iq_ref[0]
    iwt = iwt_ref[0]
    wrows = [iwt[h:h + 1, :] * (IDX_DIM ** -0.5 * N_HEADS ** -0.5) for h in range(N_HEADS)]
    iqh = [jnp.where(_group_mask(GROUP_WIDTH, HEAD_W, h), iq, 0).astype(BF16) for h in range(N_HEADS)]

    qpos = qi * tq + lax.broadcasted_iota(I32, (1, tq), 1)
    kpos = lax.broadcasted_iota(I32, (tk, 1), 0)
    causal_t = lambda c: (c * tk + kpos) <= qpos

    def score_body(c, carry):
        off = pl.multiple_of(c * tk, tk)
        ik = it[pl.ds(off, tk), :]
        sc = jnp.zeros((tk, tq), F32)
        for h in range(N_HEADS):
            sc = sc + jnp.maximum(_dot_nt(ik, iqh[h]), 0.0) * wrows[h]
        keys[c] = jnp.where(causal_t(c), _sort_key(sc), INT_MIN)
        return carry

    lax.fori_loop(0, nkv, score_body, 0)

    def rewrite(t):
        def body(c, carry):
            kc = keys[c]
            code = jnp.where(kc > t, -1, jnp.where(kc == t, c * tk + kpos, BIG))
            keys[c] = jnp.where(causal_t(c), code, BIG)
            return carry
        lax.fori_loop(0, nkv, body, 0)

    bound = _select_topk(lambda cs: _count_cols(keys, nkv, [lambda kc, c=c: kc >= c for c in cs]),
                         lambda cs: _count_cols(keys, nkv, [lambda kc, c=c: kc < c for c in cs]),
                         rewrite, (1, tq), nsel, nbits, 1)

    def select_body(c, carry):
        sel[c] = jnp.where(keys[c] <= bound, 1.0, 0.0).T
        return carry

    lax.fori_loop(0, nkv, select_body, 0)

    q = q_ref[0] * (HEAD_W ** -0.5)
    mask_fn = lambda c: sel[c] > 0.5
    mhs = [_group_mask(GROUP_WIDTH, HEAD_W, h) for h in range(N_HEADS)]
    (o,) = _flash([jnp.where(mh, q, 0).astype(BF16) for mh in mhs], kt, _head_value_loads(vt, tk), GROUP_WIDTH,
                  nkv, tk, None, mask_fn, groups=[list(range(N_HEADS))], lane_masks=mhs)
    o_ref[0] = o.astype(BF16)


def _block_means(kf, first_key, nrows=LANE):
    n = kf.shape[0]
    j = lax.broadcasted_iota(I32, (nrows, n), 0)
    s = lax.broadcasted_iota(I32, (nrows, n), 1)
    sel = jnp.where((first_key + s) // MOBA_BLOCK == j, 1.0 / MOBA_BLOCK, 0.0).astype(BF16)
    hi, lo = _split_bf16(kf)
    return _dot(sel, hi) + _dot(sel, lo)


def _moba_prompt_kernel(q_ref, kv_ref, kf_ref, o_ref, kt, vt, kmh, kml, *, tq, nsel):
    qi = pl.program_id(1)
    tk = MOBA_BLOCK

    @pl.when(qi == 0)
    def _():
        kv = kv_ref[0]
        kt[...] = _tile4(kv, 0)
        _store_head_values(vt, kv, HEAD_W)
        hi, lo = _split_bf16(_block_means(kf_ref[0], 0))
        kmh[...] = _tile4(hi, 0)
        kml[...] = _tile4(lo, 0)

    own = (qi * tq) // MOBA_BLOCK
    causal = _causal_fn(qi, tq, tk)
    q = q_ref[0]
    lane = lax.broadcasted_iota(I32, (tq, LANE), 1)
    mhs = [_group_mask(GROUP_WIDTH, HEAD_W, h) for h in range(N_HEADS)]
    qhs = [jnp.where(mh, q, 0).astype(BF16) for mh in mhs]
    mask_fns = []
    for qh in qhs:
        gate = _dot_nt(qh, kmh[...]) + _dot_nt(qh, kml[...])
        sel = jnp.where(_top_blocks(gate, own, nsel), 1.0, 0.0)

        def mask_fn(c, sel=sel):
            picked = jnp.max(jnp.where(lane == c, sel, 0.0), axis=-1, keepdims=True) > 0.5
            return (picked & (c < own)) | (causal(c) & (c >= own))

        mask_fns.append(mask_fn)
    (o,) = _flash([qh * (HEAD_W ** -0.5) for qh in qhs], kt, _head_value_loads(vt, tk), GROUP_WIDTH, own + 1, tk,
                  None, mask_fns, groups=[list(range(N_HEADS))], lane_masks=mhs)
    o_ref[0] = o.astype(BF16)


def _mla_prompt_kernel(q_ref, kv_ref, wuv_ref, o_ref, *, tq, tk):
    qi = pl.program_id(1)
    nkv = _num_chunks(qi, tq, tk)
    causal = _causal_fn(qi, tq, tk)
    kv = kv_ref.at[0]
    scale = (D_NOPE + D_ROPE) ** -0.5
    values = lambda off: kv_ref[0, pl.ds(off, tk), :]
    ols = _flash([q_ref[0, :, h * QX_W:(h + 1) * QX_W] for h in range(N_HEADS)], kv, [values] * N_HEADS, QX_W,
                 nkv, tk, scale, causal)
    o = jnp.zeros((tq, GROUP_WIDTH), F32)
    for h in range(N_HEADS):
        o = o + _dot(ols[h].astype(BF16), wuv_ref[h])
    o_ref[0] = o.astype(BF16)


def _prompt_call(kernel, name, nb, seq, tq, inputs, scratch):
    in_specs = []
    for arr, kind in inputs:
        if kind == 'q':
            in_specs.append(pl.BlockSpec((1, tq, arr.shape[-1]), lambda b, i: (b, i, 0)))
        elif kind == 'qt':
            in_specs.append(pl.BlockSpec((1, arr.shape[1], tq), lambda b, i: (b, 0, i)))
        elif kind == 'seq':
            in_specs.append(pl.BlockSpec((1, seq, arr.shape[-1]), lambda b, i: (b, 0, 0)))
        else:
            nd = arr.ndim
            in_specs.append(pl.BlockSpec(arr.shape, lambda b, i, nd=nd: (0,) * nd))
    return pl.pallas_call(
        kernel,
        grid=(nb, seq // tq),
        in_specs=in_specs,
        out_specs=pl.BlockSpec((1, tq, GROUP_WIDTH), lambda b, i: (b, i, 0)),
        out_shape=jax.ShapeDtypeStruct((nb, seq, GROUP_WIDTH), BF16),
        scratch_shapes=scratch,
        compiler_params=pltpu.CompilerParams(dimension_semantics=("arbitrary", "arbitrary"),
                                             vmem_limit_bytes=VMEM_LIMIT),
        name=name,
    )(*[a for a, _ in inputs])


def _prompt_attention(p, lw, nb, seq, lam_init):
    tq = min(256, seq)
    tk = min(512, seq)
    r3 = lambda a: a.reshape(nb, seq, a.shape[-1])
    tile = lambda: pltpu.VMEM((seq, GROUP_WIDTH), BF16)
    head_tiles = lambda: pltpu.VMEM((N_HEADS, seq, GROUP_WIDTH), BF16)
    o_a = _prompt_call(
        functools.partial(_diff_prompt_kernel, tq=tq, tk=tk, lam_init=lam_init), "diff_prompt", nb, seq, tq,
        [(r3(p['aq']), 'q'), (r3(p['ka']), 'seq'), (lw['a_lam'], 'const'), (lw['sg4'], 'const')],
        [tile(), tile()])
    nsel = min(DSA_TOPK, seq // 4)
    iwt = jnp.pad(p['iw'][:, HEAD_W:HEAD_W + N_HEADS].reshape(nb, seq, N_HEADS).transpose(0, 2, 1),
                  ((0, 0), (0, 8 - N_HEADS), (0, 0)))
    o_b = _prompt_call(
        functools.partial(_dsa_prompt_kernel, tq=tq, tk=tk, nsel=nsel, nbits=max(1, (seq - 1).bit_length())),
        "dsa_prompt", nb, seq, tq,
        [(r3(p['bq']), 'q'), (r3(p['iq']), 'q'), (iwt, 'qt'), (r3(p['kb']), 'seq')],
        [tile(), head_tiles(), tile(), pltpu.VMEM((seq // tk, tk, tq), I32), pltpu.VMEM((seq // tk, tq, tk), F32)])
    assert seq % MOBA_BLOCK == 0 and MOBA_BLOCK % tq == 0 and seq // MOBA_BLOCK <= LANE
    o_c = _prompt_call(
        functools.partial(_moba_prompt_kernel, tq=tq, nsel=min(MOBA_TOPK, seq // MOBA_BLOCK)),
        "moba_prompt", nb, seq, tq,
        [(r3(p['cq']), 'q'), (r3(p['kc']), 'seq'), (r3(p['rc']), 'seq')],
        [tile(), head_tiles(), pltpu.VMEM((LANE, GROUP_WIDTH), BF16), pltpu.VMEM((LANE, GROUP_WIDTH), BF16)])
    o_d = _prompt_call(
        functools.partial(_mla_prompt_kernel, tq=tq, tk=tk), "mla_prompt", nb, seq, tq,
        [(r3(p['qd']), 'q'), (r3(p['kd']), 'seq'), (lw['wuv'], 'const')], [])
    return [o.reshape(nb * seq, GROUP_WIDTH) for o in (o_a, o_b, o_c, o_d)]


TOK_PAD = 8
NEW_PAD = 128


def _feature_major(buf):
    return len(buf.shape) == 4


def _fetch_pages(pt_ref, cache_ref, buf, sem, *, layer, npages, page):
    b = pl.program_id(0)
    slot = b % 2

    def copy(bb, sl, j):
        if _feature_major(buf):
            dst = buf.at[sl, j]
        else:
            dst = buf.at[sl, pl.ds(pl.multiple_of(j * page, page), page), :]
        return pltpu.make_async_copy(cache_ref.at[layer, pt_ref[bb * npages + j]], dst, sem.at[sl])

    def issue(bb, sl):
        def body(j, carry):
            copy(bb, sl, j).start()
            return carry
        lax.fori_loop(0, npages, body, 0, unroll=8 if npages % 8 == 0 else 1)

    @pl.when(b == 0)
    def _():
        issue(0, 0)

    @pl.when(b + 1 < pl.num_programs(0))
    def _():
        issue(b + 1, 1 - slot)

    pltpu.make_async_copy(buf.at[slot], buf.at[slot], sem.at[slot]).wait()
    return slot


def _new_keys(new_ref, buf):
    new = new_ref[0]
    if _feature_major(buf):
        return new.astype(BF16)
    return jnp.concatenate([new, jnp.zeros((NEW_PAD - TOK_PAD, new.shape[1]), F32)], axis=0).astype(BF16)


def _key_chunk(buf, slot, c, ck):
    if _feature_major(buf):
        per = ck // buf.shape[3]
        return jnp.concatenate([buf[slot, c * per + i] for i in range(per)], axis=1).astype(BF16)
    return buf[slot, c * ck:(c + 1) * ck, :].astype(BF16)


def _qk(qx, kc, buf):
    return _dot(qx, kc) if _feature_major(buf) else _dot_nt(qx, kc)


def _pv(p, kc, buf):
    return _dot_nt(p, kc) if _feature_major(buf) else _dot(p, kc)


def _new_valid(rows, ntok):
    tok = lax.broadcasted_iota(I32, (rows, 1), 0) % TOK_PAD
    i = lax.broadcasted_iota(I32, (1, NEW_PAD), 1)
    return (i <= tok) & (i < ntok)


def _scores_to(s_ref, qx, buf, slot, new, npast, ck, scale):
    for c in range(npast // ck):
        s_ref[:, c * ck:(c + 1) * ck] = _qk(qx, _key_chunk(buf, slot, c, ck), buf) * scale
    s_ref[:, npast:npast + NEW_PAD] = _qk(qx, new, buf) * scale


def _softmax_rows(s_ref, ck):
    n = s_ref.shape[1]
    bounds = [(a, min(a + ck, n)) for a in range(0, n, ck)]
    m = jnp.full((s_ref.shape[0], 1), NEG, F32)
    for a, b in bounds:
        m = jnp.maximum(m, jnp.max(s_ref[:, a:b], axis=-1, keepdims=True))
    l = jnp.zeros((s_ref.shape[0], 1), F32)
    for a, b in bounds:
        p = jnp.exp(s_ref[:, a:b] - m)
        s_ref[:, a:b] = p
        l = l + jnp.sum(p, axis=-1, keepdims=True)
    return 1.0 / l


def _weighted_values(p_fn, buf, slot, new, npast, ck):
    acc = _pv(p_fn(npast, npast + NEW_PAD).astype(BF16), new, buf)
    for c in range(npast // ck):
        acc = acc + _pv(p_fn(c * ck, (c + 1) * ck).astype(BF16), _key_chunk(buf, slot, c, ck), buf)
    return acc


def _diff_sample_kernel(pt_ref, q_ref, new_ref, al_ref, sg_ref, cache_ref, o_ref, buf, sem, s_ref,
                        *, layer, npages, page, ck, ntok, lam_init):
    slot = _fetch_pages(pt_ref, cache_ref, buf, sem, layer=layer, npages=npages, page=page)
    npast = npages * page
    rows = 2 * N_HEADS * TOK_PAD
    half = rows // 2
    new = _new_keys(new_ref, buf)
    _scores_to(s_ref, q_ref[0], buf, slot, new, npast, ck, A_HEAD_DIM ** -0.5)
    s_ref[:, npast:] = jnp.where(_new_valid(rows, ntok), s_ref[:, npast:], NEG)
    rl = _softmax_rows(s_ref, ck)
    lam = _lambda_value(al_ref, lam_init)
    r1, r2 = rl[:half], rl[half:] * lam
    o = _weighted_values(lambda a, b: s_ref[0:half, a:b] * r1 - s_ref[half:rows, a:b] * r2, buf, slot, new, npast, ck)
    o_ref[0] = _subln(o, sg_ref[...], lam_init, o.shape[1], HEAD_W)


def _dsa_sample_kernel(pt_ref, q_ref, iq_ref, iw_ref, new_ref, cache_ref, o_ref, buf, sem, s_ref, k_ref,
                       *, layer, npages, page, ck, ntok, nsel, nbits):
    slot = _fetch_pages(pt_ref, cache_ref, buf, sem, layer=layer, npages=npages, page=page)
    npast = npages * page
    n = npast + NEW_PAD
    rows = N_HEADS * TOK_PAD
    new = _new_keys(new_ref, buf)
    iw = iw_ref[0]
    wcols = [iw[:, h:h + 1] * (IDX_DIM ** -0.5 * N_HEADS ** -0.5) for h in range(N_HEADS)]
    iqx = iq_ref[0]

    def index_scores(kc):
        si = _qk(iqx, kc, buf)
        sc = jnp.zeros((TOK_PAD, si.shape[1]), F32)
        for h in range(N_HEADS):
            sc = sc + jnp.maximum(si[h * TOK_PAD:(h + 1) * TOK_PAD], 0.0) * wcols[h]
        return _sort_key(sc)

    for c in range(npast // ck):
        k_ref[:, c * ck:(c + 1) * ck] = index_scores(_key_chunk(buf, slot, c, ck))
    new_ok = _new_valid(TOK_PAD, ntok)
    k_ref[:, npast:] = jnp.where(new_ok, index_scores(new), INT_MIN)

    def count(preds):
        nacc = 4
        accs = [[jnp.zeros((TOK_PAD, LANE), F32) for _ in range(nacc)] for _ in preds]
        for j in range(n // LANE):
            kc = k_ref[:, j * LANE:(j + 1) * LANE]
            for a, pred in zip(accs, preds):
                a[j % nacc] = a[j % nacc] + jnp.where(pred(kc), 1.0, 0.0)
        return [jnp.sum((a[0] + a[1]) + (a[2] + a[3]), axis=-1, keepdims=True) for a in accs]

    lane = lax.broadcasted_iota(I32, (1, ck), 1)

    def rewrite(t):
        for c in range(npast // ck):
            kc = k_ref[:, c * ck:(c + 1) * ck]
            k_ref[:, c * ck:(c + 1) * ck] = jnp.where(kc > t, -1, jnp.where(kc == t, c * ck + lane, BIG))
        kc = k_ref[:, npast:]
        code = jnp.where(kc > t, -1, jnp.where(kc == t, npast + lane[:, :NEW_PAD], BIG))
        k_ref[:, npast:] = jnp.where(new_ok, code, BIG)

    bound = _select_topk(lambda cs: count([lambda kc, c=c: kc >= c for c in cs]),
                         lambda cs: count([lambda kc, c=c: kc < c for c in cs]),
                         rewrite, (TOK_PAD, 1), nsel, nbits, 2)

    _scores_to(s_ref, q_ref[0], buf, slot, new, npast, ck, HEAD_W ** -0.5)
    bounds = [(a, min(a + ck, n)) for a in range(0, n, ck)]
    for a, b in bounds:
        picked = k_ref[:, a:b] <= bound
        for h in range(N_HEADS):
            r = slice(h * TOK_PAD, (h + 1) * TOK_PAD)
            s_ref[r, a:b] = jnp.where(picked, s_ref[r, a:b], NEG)
    rl = _softmax_rows(s_ref, ck)
    o_ref[0] = _weighted_values(lambda a, b: s_ref[:, a:b] * rl, buf, slot, new, npast, ck)


def _moba_sample_kernel(pt_ref, q_ref, new_ref, cache_ref, o_ref, buf, sem, s_ref,
                        *, layer, npages, page, ck, ntok, nsel):
    slot = _fetch_pages(pt_ref, cache_ref, buf, sem, layer=layer, npages=npages, page=page)
    npast = npages * page
    rows = N_HEADS * TOK_PAD
    own = npast // MOBA_BLOCK
    new = _new_keys(new_ref, buf)
    qx = q_ref[0]
    km = jnp.zeros((LANE, buf.shape[2]), F32)
    for c in range(npast // ck):
        km = km + _block_means(buf[slot, c * ck:(c + 1) * ck, :], c * ck)
    hi, lo = _split_bf16(km)
    gate = _dot_nt(qx, hi) + _dot_nt(qx, lo)
    sel = jnp.where(_top_blocks(gate, own, nsel), 1.0, 0.0).astype(BF16)
    _scores_to(s_ref, qx, buf, slot, new, npast, ck, HEAD_W ** -0.5)
    for c in range(npast // ck):
        j = lax.broadcasted_iota(I32, (LANE, ck), 0)
        s = lax.broadcasted_iota(I32, (LANE, ck), 1)
        expand = jnp.where((c * ck + s) // MOBA_BLOCK == j, 1.0, 0.0).astype(BF16)
        picked = _dot(sel, expand) > 0.5
        s_ref[:, c * ck:(c + 1) * ck] = jnp.where(picked, s_ref[:, c * ck:(c + 1) * ck], NEG)
    s_ref[:, npast:] = jnp.where(_new_valid(rows, ntok), s_ref[:, npast:], NEG)
    rl = _softmax_rows(s_ref, ck)
    o_ref[0] = _weighted_values(lambda a, b: s_ref[:, a:b] * rl, buf, slot, new, npast, ck)


def _mla_sample_kernel(pt_ref, q_ref, new_ref, wuv_ref, cache_ref, o_ref, buf, sem, s_ref,
                       *, layer, npages, page, ck, ntok):
    slot = _fetch_pages(pt_ref, cache_ref, buf, sem, layer=layer, npages=npages, page=page)
    npast = npages * page
    rows = N_HEADS * TOK_PAD
    new = _new_keys(new_ref, buf)
    _scores_to(s_ref, q_ref[0], buf, slot, new, npast, ck, (D_NOPE + D_ROPE) ** -0.5)
    s_ref[:, npast:] = jnp.where(_new_valid(rows, ntok), s_ref[:, npast:], NEG)
    rl = _softmax_rows(s_ref, ck)
    ol = _weighted_values(lambda a, b: s_ref[:, a:b] * rl, buf, slot, new, npast, ck)
    o = jnp.zeros((TOK_PAD, GROUP_WIDTH), F32)
    for h in range(N_HEADS):
        o = o + _dot(ol[h * TOK_PAD:(h + 1) * TOK_PAD].astype(BF16), wuv_ref[h])
    o_ref[0] = o


def _sample_call(kernel, name, page_table, inputs, cache, out_rows, out_w, scratch):
    nb = page_table.shape[0]
    in_specs = []
    for arr, kind in inputs:
        if kind == 'b':
            in_specs.append(pl.BlockSpec((1,) + arr.shape[1:], lambda b, pt: (b, 0, 0)))
        else:
            nd = arr.ndim
            in_specs.append(pl.BlockSpec(arr.shape, lambda b, pt, nd=nd: (0,) * nd))
    in_specs.append(pl.BlockSpec(memory_space=pl.ANY))
    return pl.pallas_call(
        kernel,
        grid_spec=pltpu.PrefetchScalarGridSpec(
            num_scalar_prefetch=1,
            grid=(nb,),
            in_specs=in_specs,
            out_specs=pl.BlockSpec((1, out_rows, out_w), lambda b, pt: (b, 0, 0)),
            scratch_shapes=scratch,
        ),
        out_shape=jax.ShapeDtypeStruct((nb, out_rows, out_w), F32),
        compiler_params=pltpu.CompilerParams(dimension_semantics=("arbitrary",), vmem_limit_bytes=VMEM_LIMIT),
        name=name,
    )(page_table.reshape(-1), *[a for a, _ in inputs], cache)


def _head_rows(q, nb, ntok, lane_lo, width):
    w = q.shape[-1] // N_HEADS
    q = q.reshape(nb, ntok, N_HEADS, w).transpose(0, 2, 1, 3)
    q = jnp.pad(q, ((0, 0), (0, 0), (0, TOK_PAD - ntok), (lane_lo, width - lane_lo - w)))
    return q.reshape(nb, N_HEADS * TOK_PAD, width)


def _unhead_rows(o, nb, ntok, lane_lo):
    o = o.reshape(nb, N_HEADS, TOK_PAD, o.shape[-1])[:, :, :ntok, lane_lo:lane_lo + HEAD_W]
    return o.transpose(0, 2, 1, 3).reshape(nb * ntok, GROUP_WIDTH).astype(BF16)


def _sample_attention(s, lw, caches, page_table, layer, ntok, lam_init):
    nb, npages = page_table.shape
    page = caches[0].shape[2]
    npast = npages * page
    ck = min(1024, npast)
    n = npast + NEW_PAD
    assert npast % ck == 0 and npast % MOBA_BLOCK == 0 and ntok <= TOK_PAD and npast // MOBA_BLOCK < LANE
    new_rows = lambda r: jnp.pad(r.reshape(nb, ntok, r.shape[-1]), ((0, 0), (0, TOK_PAD - ntok), (0, 0)))
    new_cols = lambda r: jnp.pad(r.reshape(nb, ntok, r.shape[-1]), ((0, 0), (0, NEW_PAD - ntok), (0, 0))).transpose(0, 2, 1)
    common = dict(layer=layer, npages=npages, page=page, ck=ck, ntok=ntok)
    sem = pltpu.SemaphoreType.DMA((2,))
    rows = N_HEADS * TOK_PAD

    wa = CACHE_W[0]
    aq = s['aq'].reshape(nb, ntok, N_HEADS, 2, A_HEAD_DIM).transpose(0, 3, 2, 1, 4)
    aq = jnp.pad(aq, ((0, 0), (0, 0), (0, 0), (0, TOK_PAD - ntok), (0, 0)))
    aq = jnp.stack([jnp.pad(aq[:, m], ((0, 0), (0, 0), (0, 0), (m * A_HEAD_DIM, wa - (m + 1) * A_HEAD_DIM)))
                    for m in range(2)], axis=1).reshape(nb, 2 * rows, wa)
    o_a = _sample_call(
        functools.partial(_diff_sample_kernel, lam_init=lam_init, **common), "diff_sample", page_table,
        [(aq, 'b'), (new_rows(s['ra']), 'b'), (lw['a_lam'], 'const'), (lw['sg_pad'], 'const')], caches[0], rows, wa,
        [pltpu.VMEM((2, npast, wa), F32), sem, pltpu.VMEM((2 * rows, n), F32)])

    wb = CACHE_W[1]
    iw = jnp.pad(s['iw'][:, HEAD_W:HEAD_W + N_HEADS].reshape(nb, ntok, N_HEADS),
                 ((0, 0), (0, TOK_PAD - ntok), (0, LANE - N_HEADS)))
    nsel = min(DSA_TOPK, (npast + ntok) // 4)
    o_b = _sample_call(
        functools.partial(_dsa_sample_kernel, nsel=nsel, nbits=max(1, (n - 1).bit_length()), **common),
        "dsa_sample", page_table,
        [(_head_rows(s['bq'], nb, ntok, 0, wb), 'b'), (_head_rows(s['iq'], nb, ntok, 2 * HEAD_W, wb), 'b'),
         (iw, 'b'), (new_cols(s['rb']), 'b')], caches[1], rows, wb,
        [pltpu.VMEM((2, npages, wb, page), F32), sem, pltpu.VMEM((rows, n), F32), pltpu.VMEM((TOK_PAD, n), I32)])

    wc = CACHE_W[2]
    o_c = _sample_call(
        functools.partial(_moba_sample_kernel, nsel=min(MOBA_TOPK, npast // MOBA_BLOCK + 1), **common),
        "moba_sample", page_table,
        [(_head_rows(s['cq'], nb, ntok, 0, wc), 'b'), (new_rows(s['rc']), 'b')], caches[2], rows, wc,
        [pltpu.VMEM((2, npast, wc), F32), sem, pltpu.VMEM((rows, n), F32)])

    wd = CACHE_W[3]
    qd = s['qd'].reshape(nb * ntok, N_HEADS, QX_W)[:, :, :wd].reshape(nb * ntok, N_HEADS * wd)
    o_d = _sample_call(
        functools.partial(_mla_sample_kernel, **common), "mla_sample", page_table,
        [(_head_rows(qd, nb, ntok, 0, wd), 'b'), (new_cols(s['rd']), 'b'), (lw['wuv_s'], 'const')], caches[3],
        TOK_PAD, GROUP_WIDTH,
        [pltpu.VMEM((2, npages, wd, page), F32), sem, pltpu.VMEM((rows, n), F32)])

    return [_unhead_rows(o_a, nb, ntok, HEAD_W), _unhead_rows(o_b, nb, ntok, HEAD_W),
            _unhead_rows(o_c, nb, ntok, HEAD_W),
            o_d[:, :ntok].reshape(nb * ntok, GROUP_WIDTH).astype(BF16)]


def _prep_weights(attn_norm_g, w_in, a_lambda, a_subln_g, d_q_norm_g, d_kv_norm_g, d_w_uq, d_w_uk, d_w_uv,
                  w_out, ffn_norm_g, w_gate, w_up, w_down):
    depth = w_in.shape[0]
    src, swp, _, _ = _build_layout()
    w_pad = jnp.pad(w_in, ((0, 0), (0, 0), (0, 1)))
    w_cat = jnp.take(w_pad, jnp.asarray(np.concatenate([src, swp])), axis=2).astype(BF16)
    qsrc, qswp, _, _ = _build_q_layout()
    uq_pad = jnp.pad(d_w_uq, ((0, 0), (0, 0), (0, 1)))
    wuq = jnp.take(uq_pad, jnp.asarray(np.concatenate([qsrc, qswp])), axis=2).astype(BF16)
    eye_h = jnp.eye(N_HEADS, dtype=F32)
    nope = jnp.einsum('lchn,hg->lhngc', d_w_uk, eye_h)
    nope = jnp.pad(nope, ((0, 0),) * 4 + ((0, QX_W - KV_LORA),)).reshape(depth, N_HEADS * D_NOPE, N_HEADS * QX_W)
    rope = np.zeros((N_HEADS, D_ROPE, N_HEADS, QX_W), np.float32)
    for h in range(N_HEADS):
        rope[h, np.arange(D_ROPE), h, KV_LORA + np.arange(D_ROPE)] = 1.0
    rope = jnp.broadcast_to(jnp.asarray(rope.reshape(N_HEADS * D_ROPE, N_HEADS * QX_W))[None],
                            (depth, N_HEADS * D_ROPE, N_HEADS * QX_W))
    bd = jnp.concatenate([nope, rope], axis=1).astype(BF16)
    wuv = jnp.einsum('lchv,hg->lhcgv', d_w_uv, eye_h).reshape(depth, N_HEADS, KV_LORA, GROUP_WIDTH)
    wuv_s = jnp.pad(wuv, ((0, 0), (0, 0), (0, CACHE_W[3] - KV_LORA), (0, 0))).astype(BF16)
    wuv = jnp.pad(wuv, ((0, 0), (0, 0), (0, QX_W - KV_LORA), (0, 0))).astype(BF16)
    layers = []
    for l in range(depth):
        layers.append(dict(
            attn_g=attn_norm_g[l][None], w_cat=w_cat[l], gq=d_q_norm_g[l][None], gkv=d_kv_norm_g[l][None],
            wuq=wuq[l], bd=bd[l], wuv=wuv[l], wuv_s=wuv_s[l], a_lam=a_lambda[l],
            sg4=jnp.tile(a_subln_g[l], N_HEADS)[None],
            sg_pad=jnp.pad(a_subln_g[l], (HEAD_W, 0))[None],
            wo=w_out[l].astype(BF16), ffn_g=ffn_norm_g[l][None],
            wg=w_gate[l].astype(BF16), wu=w_up[l].astype(BF16), wd=w_down[l].astype(BF16)))
    return layers


def kernel(x_prompt, x_sample, cache_diff_kv, cache_dsa_kv, cache_moba_kv, cache_mla_latent, page_table,
           attn_norm_g, w_in, a_lambda, a_subln_g, d_q_norm_g, d_kv_norm_g, d_w_uq, d_w_uk, d_w_uv,
           w_out, ffn_norm_g, w_gate, w_up, w_down, final_norm_g):
    nb, seq, d = x_prompt.shape
    db, ntok, _ = x_sample.shape
    depth = w_in.shape[0]
    caches = (cache_diff_kv, jnp.swapaxes(cache_dsa_kv, 2, 3), cache_moba_kv, jnp.swapaxes(cache_mla_latent, 2, 3))
    past_len = page_table.shape[1] * cache_diff_kv.shape[2]
    layers = _prep_weights(attn_norm_g, w_in, a_lambda, a_subln_g, d_q_norm_g, d_kv_norm_g, d_w_uq, d_w_uk,
                           d_w_uv, w_out, ffn_norm_g, w_gate, w_up, w_down)
    _, _, fidx, sgn = _build_layout()
    _, _, qfidx, qsgn = _build_q_layout()
    pos_p = jnp.arange(seq, dtype=jnp.int32)
    pos_s = jnp.tile(past_len + jnp.arange(ntok, dtype=jnp.int32), db)
    tabs_p = _rope_tables(pos_p, fidx, sgn) + _rope_tables(pos_p, qfidx, qsgn)
    tabs_s = _rope_tables(pos_s, fidx, sgn) + _rope_tables(pos_s, qfidx, qsgn)
    tabs_p = (tabs_p[0], tabs_p[1], tabs_p[2], tabs_p[3])
    tabs_s = (tabs_s[0], tabs_s[1], tabs_s[2], tabs_s[3])
    ntab_p = seq // _token_tile(nb * seq) if seq % _token_tile(nb * seq) == 0 else None
    assert ntab_p is not None
    ntab_s = (db * ntok) // _token_tile(db * ntok)

    hp = x_prompt.reshape(nb * seq, d)
    hs = x_sample.reshape(db * ntok, d)
    rows_p = {k: [] for k in ('ra', 'rb', 'rc', 'rd')}
    rows_s = {k: [] for k in ('ra', 'rb', 'rc', 'rd')}
    y_p = y_s = None
    for l in range(depth):
        lam_init = 0.8 - 0.6 * math.exp(-0.3 * l)
        lw = layers[l]
        fin = final_norm_g[None] if l == depth - 1 else None
        p = _inproj(hp, tabs_p, lw, ntab_p)
        o_p = _prompt_attention(p, lw, nb, seq, lam_init)
        hp, y_p = _ffn(hp, o_p, lw, fin)
        s = _inproj(hs, tabs_s, lw, ntab_s)
        o_s = _sample_attention(s, lw, caches, page_table, l, ntok, lam_init)
        hs, y_s = _ffn(hs, o_s, lw, fin)
        for k in rows_p:
            rows_p[k].append(p[k].reshape(nb, seq, -1))
            rows_s[k].append(s[k].reshape(db, ntok, -1))
    stack = lambda xs: jnp.stack(xs)
    return (y_p.reshape(nb, seq, d), y_s.reshape(db, ntok, d),
            stack(rows_p['ra']), stack(rows_p['rb']), stack(rows_p['rc']), stack(rows_p['rd']),
            stack(rows_s['ra']), stack(rows_s['rb']), stack(rows_s['rc']), stack(rows_s['rd']))
```
